```python
import math
import jax, jax.numpy as jnp
from jax import lax
import numpy as np

D_MODEL = 2048
BATCH = 4
SEQ = 4096
DEPTH = 4

CHUNK = 64
Q_BLOCK = 128
ROPE_THETA = 500000.0
NORM_EPS = 1e-6
NEG_INF = -1e30
PLE_DIM = 256
HEAD_DIM = 128

A_HEADS = 6
A_Q_RANK = 384
A_KV_RANK = 256
A_NOPE = 128
A_ROPE = 64
A_V = 128
A_WIDTH = A_HEADS * A_V
B_HEADS = 5
B_WIDTH = B_HEADS * HEAD_DIM
IDX_HEADS = 8
IDX_DIM = 64
TOPK_MAX = 256
C_HEADS = 5
C_QK = 64
C_V = 128
C_WIDTH = C_HEADS * C_V
D_MIX = A_WIDTH + B_WIDTH + C_WIDTH

IN_SPLITS = (A_Q_RANK, A_KV_RANK, A_ROPE, A_WIDTH,
             B_WIDTH, B_WIDTH, B_WIDTH, IDX_HEADS * IDX_DIM, IDX_DIM, IDX_HEADS, B_WIDTH,
             C_HEADS * 2 * C_QK, C_HEADS * 2 * C_QK, C_WIDTH, C_WIDTH)
D_IN = sum(IN_SPLITS)

kernel_name = "hybrid_mla_dsa_diff_streaming_block"


def rms_norm(x, g):
    xf = x.astype(jnp.float32)
    y = xf * lax.rsqrt(jnp.mean(xf * xf, axis=-1, keepdims=True) + NORM_EPS)
    return (y * g.astype(jnp.float32)).astype(x.dtype)


def rope(x, pos, n_rot):
    half = n_rot // 2
    inv_freq = 1.0 / (ROPE_THETA ** (jnp.arange(half, dtype=jnp.float32) * (2.0 / n_rot)))
    ang = pos.astype(jnp.float32)[:, :, None] * inv_freq
    cos = jnp.cos(ang)[:, :, None, :]
    sin = jnp.sin(ang)[:, :, None, :]
    x1 = x[..., :half].astype(jnp.float32)
    x2 = x[..., half:n_rot].astype(jnp.float32)
    rot = jnp.concatenate([x1 * cos - x2 * sin, x2 * cos + x1 * sin], axis=-1).astype(x.dtype)
    return jnp.concatenate([rot, x[..., n_rot:]], axis=-1)


def split_cols(z, sizes):
    out, o = [], 0
    for sz in sizes:
        out.append(z[..., o:o + sz])
        o += sz
    return out


def to_blocks(a):
    b, s = a.shape[:2]
    return jnp.swapaxes(a.reshape(b, s // Q_BLOCK, Q_BLOCK, *a.shape[2:]), 0, 1)


def from_blocks(a):
    nb, b, qb = a.shape[:3]
    return jnp.swapaxes(a, 0, 1).reshape(b, nb * qb, *a.shape[3:])


def chunk_mask(blk, seq):
    q_pos = blk * Q_BLOCK + jnp.arange(Q_BLOCK)
    k_pos = jnp.arange(seq)
    return (k_pos // CHUNK)[None, :] <= (q_pos // CHUNK)[:, None]


def masked_softmax(scores, mask):
    return jax.nn.softmax(jnp.where(mask, scores.astype(jnp.float32), NEG_INF), axis=-1)


def mla_mixer(c_q, c_kv, k_rope, pos, q_norm_g, kv_norm_g, w_uq, w_ukv):
    b, s, _ = c_q.shape
    q = (rms_norm(c_q, q_norm_g) @ w_uq).reshape(b, s, A_HEADS, A_NOPE + A_ROPE)
    q = jnp.concatenate([q[..., :A_NOPE], rope(q[..., A_NOPE:], pos, A_ROPE)], axis=-1)
    kv = (rms_norm(c_kv, kv_norm_g) @ w_ukv).reshape(b, s, A_HEADS, A_NOPE + A_V)
    k_pe = rope(k_rope[:, :, None, :], pos, A_ROPE)
    k = jnp.concatenate([kv[..., :A_NOPE], jnp.broadcast_to(k_pe, (b, s, A_HEADS, A_ROPE))], axis=-1)
    v = kv[..., A_NOPE:]
    scale = (A_NOPE + A_ROPE) ** -0.5

    def block(args):
        blk, qb = args
        sc = jnp.einsum('bqhd,bkhd->bhqk', qb, k) * scale
        pr = masked_softmax(sc, chunk_mask(blk, s)).astype(v.dtype)
        return jnp.einsum('bhqk,bkhd->bqhd', pr, v)

    o = from_blocks(lax.map(block, (jnp.arange(s // Q_BLOCK), to_blocks(q))))
    return o.reshape(b, s, A_WIDTH)


def dsa_mixer(q, k, v, q_idx, k_idx, w_idx, pos):
    b, s, _ = q.shape
    q = rope(q.reshape(b, s, B_HEADS, HEAD_DIM), pos, HEAD_DIM // 4)
    k = rope(k.reshape(b, s, B_HEADS, HEAD_DIM), pos, HEAD_DIM // 4)
    v = v.reshape(b, s, B_HEADS, HEAD_DIM)
    qi = rope(q_idx.reshape(b, s, IDX_HEADS, IDX_DIM), pos, IDX_DIM // 4)
    ki = rope(k_idx[:, :, None, :], pos, IDX_DIM // 4)[:, :, 0, :]
    n_sel = min(TOPK_MAX, s // 4)
    scale = HEAD_DIM ** -0.5

    def block(args):
        blk, qb, qib, wb = args
        mask = chunk_mask(blk, s)
        rel = jax.nn.relu(jnp.einsum('bqhd,bkd->bqhk', qib, ki).astype(jnp.float32))
        score = jnp.einsum('bqhk,bqh->bqk', rel, wb.astype(jnp.float32))
        score = jnp.where(mask[None], score, NEG_INF)
        top_val, top_idx = lax.top_k(score, n_sel)
        valid = top_val > 0.5 * NEG_INF
        kg = jax.vmap(lambda kk, ii: kk[ii])(k, top_idx)
        vg = jax.vmap(lambda vv, ii: vv[ii])(v, top_idx)
        sc = jnp.einsum('bqhd,bqkhd->bhqk', qb, kg) * scale
        pr = masked_softmax(sc, valid[:, None]).astype(v.dtype)
        return jnp.einsum('bhqk,bqkhd->bqhd', pr, vg)

    o = from_blocks(lax.map(block, (jnp.arange(s // Q_BLOCK), to_blocks(q), to_blocks(qi), to_blocks(w_idx))))
    return o.reshape(b, s, B_WIDTH)


def diff_mixer(q, k, v, pos, lam, lam_init, subln_g):
    b, s, _ = q.shape
    q = rope(q.reshape(b, s, C_HEADS * 2, C_QK), pos, C_QK // 4).reshape(b, s, C_HEADS, 2, C_QK)
    k = rope(k.reshape(b, s, C_HEADS * 2, C_QK), pos, C_QK // 4).reshape(b, s, C_HEADS, 2, C_QK)
    v = v.reshape(b, s, C_HEADS, C_V)
    k1, k2 = k[..., 0, :], k[..., 1, :]
    scale = C_QK ** -0.5

    def block(args):
        blk, qb = args
        mask = chunk_mask(blk, s)
        a1 = masked_softmax(jnp.einsum('bqhd,bkhd->bhqk', qb[..., 0, :], k1) * scale, mask)
        a2 = masked_softmax(jnp.einsum('bqhd,bkhd->bhqk', qb[..., 1, :], k2) * scale, mask)
        pr = (a1 - lam * a2).astype(v.dtype)
        return jnp.einsum('bhqk,bkhd->bqhd', pr, v)

    o = from_blocks(lax.map(block, (jnp.arange(s // Q_BLOCK), to_blocks(q))))
    o = rms_norm(o, subln_g) * (1.0 - lam_init)
    return o.reshape(b, s, C_WIDTH)


def setup_inputs(seed: int = 0) -> dict:
    key = jax.random.key(seed)
    ks = jax.random.split(key, 20)
    f32 = jnp.float32
    nrm = lambda k, shape, sc: jax.random.normal(k, shape, f32) * sc
    x = nrm(ks[0], (BATCH, SEQ, D_MODEL), 1.0)
    p = nrm(ks[1], (DEPTH, BATCH, SEQ, PLE_DIM), 1.0)
    start = jax.random.randint(ks[2], (BATCH, 1), 0, 64, dtype=jnp.int32) * CHUNK
    positions = (start + jnp.arange(SEQ, dtype=jnp.int32)[None, :]).astype(jnp.int32)
    return {
        "x": x,
        "p": p,
        "positions": positions,
        "w_in": nrm(ks[3], (DEPTH, D_MODEL, D_IN), D_MODEL ** -0.5),
        "w_uq": nrm(ks[4], (DEPTH, A_Q_RANK, A_HEADS * (A_NOPE + A_ROPE)), A_Q_RANK ** -0.5),
        "w_ukv": nrm(ks[5], (DEPTH, A_KV_RANK, A_HEADS * (A_NOPE + A_V)), A_KV_RANK ** -0.5),
        "w_o": nrm(ks[6], (DEPTH, D_MIX, D_MODEL), D_MIX ** -0.5),
        "norm_g": 1.0 + nrm(ks[7], (DEPTH, D_MODEL), 0.02),
        "q_norm_g": 1.0 + nrm(ks[8], (DEPTH, A_Q_RANK), 0.02),
        "kv_norm_g": 1.0 + nrm(ks[9], (DEPTH, A_KV_RANK), 0.02),
        "lam_q1": nrm(ks[10], (DEPTH, C_QK), 0.1),
        "lam_k1": nrm(ks[11], (DEPTH, C_QK), 0.1),
        "lam_q2": nrm(ks[12], (DEPTH, C_QK), 0.1),
        "lam_k2": nrm(ks[13], (DEPTH, C_QK), 0.1),
        "subln_g": 1.0 + nrm(ks[14], (DEPTH, C_V), 0.02),
        "w_ple": nrm(ks[15], (DEPTH, PLE_DIM, D_MODEL), PLE_DIM ** -0.5),
        "w_pg": nrm(ks[16], (DEPTH, D_MODEL, D_MODEL), D_MODEL ** -0.5),
        "final_g": 1.0 + nrm(ks[17], (D_MODEL,), 0.02),
    }


def reference(x, p, positions, w_in, w_uq, w_ukv, w_o, norm_g, q_norm_g, kv_norm_g,
              lam_q1, lam_k1, lam_q2, lam_k2, subln_g, w_ple, w_pg, final_g):
    h = x
    f32 = jnp.float32
    for i in range(DEPTH):
        u = rms_norm(h, norm_g[i])
        z = u @ w_in[i]
        (c_q, c_kv, k_rope, g_a,
         q_b, k_b, v_b, q_idx, k_idx, w_idx, g_b,
         q_c, k_c, v_c, g_c) = split_cols(z, IN_SPLITS)
        o_a = mla_mixer(c_q, c_kv, k_rope, positions, q_norm_g[i], kv_norm_g[i], w_uq[i], w_ukv[i])
        o_b = dsa_mixer(q_b, k_b, v_b, q_idx, k_idx, w_idx, positions)
        lam_init = 0.8 - 0.6 * math.exp(-0.3 * i)
        lam = (jnp.exp(jnp.sum(lam_q1[i].astype(f32) * lam_k1[i].astype(f32)))
               - jnp.exp(jnp.sum(lam_q2[i].astype(f32) * lam_k2[i].astype(f32))) + lam_init)
        o_c = diff_mixer(q_c, k_c, v_c, positions, lam, lam_init, subln_g[i])
        mixed = jnp.concatenate([o_a * jax.nn.silu(g_a),
                                 o_b * jax.nn.silu(g_b),
                                 o_c * jax.nn.silu(g_c)], axis=-1)
        h = h + mixed @ w_o[i]
        h = h + (p[i] @ w_ple[i]) * jax.nn.sigmoid(h @ w_pg[i])
    return rms_norm(h, final_g)
```

```python
import functools
import math

import jax
import jax.numpy as jnp
import numpy as np
from jax import lax
from jax.experimental import pallas as pl
from jax.experimental.pallas import tpu as pltpu

D_MODEL = 2048
DEPTH = 4
CHUNK = 64
ROPE_THETA = 500000.0
NORM_EPS = 1e-6
NEG_INF = -1e30
PLE_DIM = 256
HEAD_DIM = 128

A_HEADS = 6
A_Q_RANK = 384
A_KV_RANK = 256
A_NOPE = 128
A_ROPE = 64
A_V = 128
A_WIDTH = A_HEADS * A_V
B_HEADS = 5
B_WIDTH = B_HEADS * HEAD_DIM
IDX_HEADS = 8
IDX_DIM = 64
TOPK_MAX = 256
C_HEADS = 5
C_QK = 64
C_V = 128
C_WIDTH = C_HEADS * C_V
D_MIX = A_WIDTH + B_WIDTH + C_WIDTH

IN_NAMES = ("c_q", "c_kv", "k_rope", "g_a", "q_b", "k_b", "v_b", "q_idx", "k_idx", "w_idx", "g_b",
            "q_c", "k_c", "v_c", "g_c")
IN_SPLITS = (A_Q_RANK, A_KV_RANK, A_ROPE, A_WIDTH,
             B_WIDTH, B_WIDTH, B_WIDTH, IDX_HEADS * IDX_DIM, IDX_DIM, IDX_HEADS, B_WIDTH,
             C_HEADS * 2 * C_QK, C_HEADS * 2 * C_QK, C_WIDTH, C_WIDTH)

LANES = 128
A_QK_PAD = 256
INT_MIN = -2 ** 31
MXU_DTYPE = jnp.bfloat16
VMEM_LIMIT = 56 * 1024 * 1024

ROPE_PATTERNS = ((A_ROPE, 64), (HEAD_DIM // 4, 128), (IDX_DIM // 4, 64))
PAT_MLA, PAT_DSA, PAT_SMALL = 0, 1, 2

_NT = (((1,), (1,)), ((), ()))


def _params(n_axes):
    return pltpu.CompilerParams(dimension_semantics=("arbitrary",) * n_axes,
                                vmem_limit_bytes=VMEM_LIMIT)


def _dot(a, b):
    return jnp.dot(a, b, preferred_element_type=jnp.float32)


def _dot_nt(a, b):
    return lax.dot_general(a, b, _NT, preferred_element_type=jnp.float32)


def _sigmoid(x):
    return 1.0 / (1.0 + jnp.exp(-x))


def _rms(x, g):
    return x * lax.rsqrt(jnp.mean(x * x, axis=-1, keepdims=True) + NORM_EPS) * g


def _rope_pattern_rows():
    freq = []
    m_up = np.zeros((3, LANES), np.float32)
    m_dn = np.zeros((3, LANES), np.float32)
    for p, (n_rot, period) in enumerate(ROPE_PATTERNS):
        half = n_rot // 2
        inv = 1.0 / (ROPE_THETA ** (jnp.arange(half, dtype=jnp.float32) * (2.0 / n_rot)))
        one = jnp.concatenate([inv, inv, jnp.zeros((period - n_rot,), jnp.float32)])
        freq.append(jnp.tile(one, LANES // period))
        for lane in range(LANES):
            m = lane % period
            m_up[p, lane] = float(m < half)
            m_dn[p, lane] = float(half <= m < n_rot)
    return jnp.stack(freq), jnp.asarray(m_up), jnp.asarray(m_dn)


def _rope_tab_kernel(pos_ref, f_ref, mu_ref, md_ref, c_ref, sa_ref, sb_ref):
    pos = pos_ref[...]
    for p in range(3):
        ang = pos * f_ref[p:p + 1, :]
        sin = jnp.sin(ang)
        c_ref[p] = jnp.cos(ang)
        sa_ref[p] = -sin * mu_ref[p:p + 1, :]
        sb_ref[p] = sin * md_ref[p:p + 1, :]


def _rope_tables(pos_f32):
    n = pos_f32.shape[0]
    tm = min(n, 1024)
    freq, m_up, m_dn = _rope_pattern_rows()
    row = pl.BlockSpec((3, LANES), lambda i: (0, 0))
    tab = pl.BlockSpec((3, tm, LANES), lambda i: (0, i, 0))
    shape = jax.ShapeDtypeStruct((3, n, LANES), jnp.float32)
    return pl.pallas_call(
        _rope_tab_kernel,
        grid=(n // tm,),
        in_specs=[pl.BlockSpec((tm, 1), lambda i: (i, 0)), row, row, row],
        out_specs=[tab, tab, tab],
        out_shape=[shape, shape, shape],
        compiler_params=_params(1),
        name="rope_tables",
    )(pos_f32, freq, m_up, m_dn)


def _rope128(x, c, sa, sb, half):
    return x * c + pltpu.roll(x, LANES - half, 1) * sa + pltpu.roll(x, half, 1) * sb


def _norm_kernel(x_ref, g_ref, o_ref):
    o_ref[...] = _rms(x_ref[...], g_ref[...]).astype(o_ref.dtype)


def _norm(x, g, out_dtype):
    n, d = x.shape
    tm = min(n, 512)
    return pl.pallas_call(
        _norm_kernel,
        grid=(n // tm,),
        in_specs=[pl.BlockSpec((tm, d), lambda i: (i, 0)), pl.BlockSpec((1, d), lambda i: (0, 0))],
        out_specs=pl.BlockSpec((tm, d), lambda i: (i, 0)),
        out_shape=jax.ShapeDtypeStruct((n, d), out_dtype),
        compiler_params=_params(1),
        name="rms_norm",
    )(x, g.reshape(1, d))


def _proj_kernel(*refs, segs, with_tables):
    u_ref, w_ref = refs[0], refs[1]
    if with_tables:
        c_ref, sa_ref, sb_ref = refs[2:5]
        outs = refs[5:]
    else:
        outs = refs[2:]
    u = u_ref[...]
    for (c0, width, pat, scale), o_ref in zip(segs, outs):
        z = _dot(u, w_ref[:, c0:c0 + width])
        if pat is None and scale is None:
            o_ref[...] = z.astype(o_ref.dtype)
            continue
        for ch in range(width // LANES):
            sl = slice(ch * LANES, (ch + 1) * LANES)
            x = z[:, sl]
            if pat is not None:
                x = _rope128(x, c_ref[pat], sa_ref[pat], sb_ref[pat], ROPE_PATTERNS[pat][0] // 2)
            if scale is not None:
                x = x * scale
            o_ref[:, sl] = x.astype(o_ref.dtype)


def _proj(u, w, tabs, segs, out_dtypes, name):
    n, d = u.shape
    tm = min(n, 512)
    width = w.shape[1]
    in_specs = [pl.BlockSpec((tm, d), lambda i: (i, 0)), pl.BlockSpec((d, width), lambda i: (0, 0))]
    args = [u, w]
    if tabs is not None:
        in_specs += [pl.BlockSpec((3, tm, LANES), lambda i: (0, i, 0))] * 3
        args += list(tabs)
    out_specs = [pl.BlockSpec((tm, s[1]), lambda i: (i, 0)) for s in segs]
    out_shape = [jax.ShapeDtypeStruct((n, s[1]), dt) for s, dt in zip(segs, out_dtypes)]
    return pl.pallas_call(
        functools.partial(_proj_kernel, segs=tuple(segs), with_tables=tabs is not None),
        grid=(n // tm,),
        in_specs=in_specs,
        out_specs=out_specs,
        out_shape=out_shape,
        compiler_params=_params(1),
        name=name,
    )(*args)


def _mla_proj_kernel(u_ref, w_ref, qg_ref, kvg_ref, wuq_ref, wukv_ref, c_ref, sa_ref, sb_ref,
                     q_ref, k_ref, v_ref):
    z = _dot(u_ref[...], w_ref[...])
    c, sa, sb = c_ref[PAT_MLA], sa_ref[PAT_MLA], sb_ref[PAT_MLA]
    half = A_ROPE // 2
    scale = (A_NOPE + A_ROPE) ** -0.5
    c_q = z[:, :A_Q_RANK]
    c_kv = z[:, A_Q_RANK:A_Q_RANK + A_KV_RANK]
    k_rope = z[:, A_Q_RANK + A_KV_RANK:]
    q = _dot(_rms(c_q, qg_ref[...]).astype(MXU_DTYPE), wuq_ref[...])
    kv = _dot(_rms(c_kv, kvg_ref[...]).astype(MXU_DTYPE), wukv_ref[...])
    k_pe = _rope128(k_rope, c, sa, sb, half).astype(k_ref.dtype)
    for h in range(A_HEADS):
        o = h * A_QK_PAD
        q_ref[:, o:o + A_NOPE] = (q[:, o:o + A_NOPE] * scale).astype(q_ref.dtype)
        q_rot = _rope128(q[:, o + A_NOPE:o + A_QK_PAD], c, sa, sb, half)
        q_ref[:, o + A_NOPE:o + A_QK_PAD] = (q_rot * scale).astype(q_ref.dtype)
        k_ref[:, o:o + A_NOPE] = kv[:, h * A_NOPE:(h + 1) * A_NOPE].astype(k_ref.dtype)
        k_ref[:, o + A_NOPE:o + A_QK_PAD] = k_pe
    v_ref[...] = kv[:, A_WIDTH:].astype(v_ref.dtype)


def _mla_proj(u, w, qg, kvg, wuq, wukv, tabs):
    n, d = u.shape
    tm = min(n, 512)
    full = lambda a: pl.BlockSpec(a.shape, lambda i: (0,) * a.ndim)
    tab = pl.BlockSpec((3, tm, LANES), lambda i: (0, i, 0))
    qk = A_HEADS * A_QK_PAD
    return pl.pallas_call(
        _mla_proj_kernel,
        grid=(n // tm,),
        in_specs=[pl.BlockSpec((tm, d), lambda i: (i, 0)), full(w), full(qg), full(kvg), full(wuq),
                  full(wukv), tab, tab, tab],
        out_specs=[pl.BlockSpec((tm, qk), lambda i: (i, 0)), pl.BlockSpec((tm, qk), lambda i: (i, 0)),
                   pl.BlockSpec((tm, A_WIDTH), lambda i: (i, 0))],
        out_shape=[jax.ShapeDtypeStruct((n, qk), MXU_DTYPE), jax.ShapeDtypeStruct((n, qk), MXU_DTYPE),
                   jax.ShapeDtypeStruct((n, A_WIDTH), MXU_DTYPE)],
        compiler_params=_params(1),
        name="mla_proj",
    )(u, w, qg, kvg, wuq, wukv, *tabs)


def _diag_mask(tq, tk, q0, k0):
    qc = (q0 + lax.broadcasted_iota(jnp.int32, (tq, tk), 0)) // CHUNK
    kc = (k0 + lax.broadcasted_iota(jnp.int32, (tq, tk), 1)) // CHUNK
    return kc <= qc


def _online_softmax_step(s, v, m_ref, l_ref, acc_ref):
    m_prev = m_ref[...]
    m_new = jnp.maximum(m_prev, jnp.max(s, axis=-1, keepdims=True))
    alpha = jnp.exp(m_prev - m_new)
    p = jnp.exp(s - m_new)
    l_ref[...] = alpha * l_ref[...] + jnp.sum(p, axis=-1, keepdims=True)
    acc_ref[...] = alpha * acc_ref[...] + _dot(p.astype(v.dtype), v)
    m_ref[...] = m_new


def _mla_attn_kernel(q_ref, k_ref, v_ref, o_ref, m_sc, l_sc, acc_sc, *, tq):
    i = pl.program_id(2)
    q = q_ref[...]
    m_sc[...] = jnp.full(m_sc.shape, NEG_INF, jnp.float32)
    l_sc[...] = jnp.zeros(l_sc.shape, jnp.float32)
    acc_sc[...] = jnp.zeros(acc_sc.shape, jnp.float32)

    def tile(j, masked):
        rows = pl.ds(pl.multiple_of(j * tq, tq), tq)
        s = _dot_nt(q, k_ref[rows, :])
        if masked:
            s = jnp.where(_diag_mask(tq, tq, 0, 0), s, NEG_INF)
        _online_softmax_step(s, v_ref[rows, :], m_sc, l_sc, acc_sc)

    def body(j, carry):
        tile(j, False)
        return carry

    lax.fori_loop(0, i, body, 0)
    tile(i, True)
    o_ref[...] = acc_sc[...] / l_sc[...]


def _mla_attn(q, k, v):
    b, s, _ = q.shape
    tq = min(s, 256)
    return pl.pallas_call(
        functools.partial(_mla_attn_kernel, tq=tq),
        grid=(b, A_HEADS, s // tq),
        in_specs=[pl.BlockSpec((None, tq, A_QK_PAD), lambda bi, h, i: (bi, i, h)),
                  pl.BlockSpec((None, s, A_QK_PAD), lambda bi, h, i: (bi, 0, h)),
                  pl.BlockSpec((None, s, A_V), lambda bi, h, i: (bi, 0, h))],
        out_specs=pl.BlockSpec((None, tq, A_V), lambda bi, h, i: (bi, i, h)),
        out_shape=jax.ShapeDtypeStruct((b, s, A_WIDTH), jnp.float32),
        scratch_shapes=[pltpu.VMEM((tq, 1), jnp.float32), pltpu.VMEM((tq, 1), jnp.float32),
                        pltpu.VMEM((tq, A_V), jnp.float32)],
        compiler_params=_params(3),
        name="mla_attn",
    )(q, k, v)


def _diff_attn_kernel(q_ref, k_ref, v_ref, c_ref, o_ref, m_sc, l_sc, acc_sc, *, tq):
    i = pl.program_id(2)
    q = q_ref[...]
    lane = lax.broadcasted_iota(jnp.int32, q.shape, 1)
    zero = jnp.zeros_like(q)
    qs = (jnp.where(lane < C_QK, q, zero), jnp.where(lane >= C_QK, q, zero))
    m_sc[...] = jnp.full(m_sc.shape, NEG_INF, jnp.float32)
    l_sc[...] = jnp.zeros(l_sc.shape, jnp.float32)
    acc_sc[...] = jnp.zeros(acc_sc.shape, jnp.float32)

    def tile(j, masked):
        rows = pl.ds(pl.multiple_of(j * tq, tq), tq)
        k = k_ref[rows, :]
        v = v_ref[rows, :]
        for t in range(2):
            s = _dot_nt(qs[t], k)
            if masked:
                s = jnp.where(_diag_mask(tq, tq, 0, 0), s, NEG_INF)
            _online_softmax_step(s, v, m_sc.at[t], l_sc.at[t], acc_sc.at[t])

    def body(j, carry):
        tile(j, False)
        return carry

    lax.fori_loop(0, i, body, 0)
    tile(i, True)

    c = c_ref[...]
    lam_init = c[4:5, 0:1]
    lam = (jnp.exp(jnp.sum(c[0:1] * c[1:2], axis=-1, keepdims=True))
           - jnp.exp(jnp.sum(c[2:3] * c[3:4], axis=-1, keepdims=True)) + lam_init)
    o = acc_sc[0] / l_sc[0] - lam * (acc_sc[1] / l_sc[1])
    o_ref[...] = _rms(o, c[5:6]) * (1.0 - lam_init)


def _diff_attn(q, k, v, consts):
    b, s, _ = q.shape
    tq = min(s, 256)
    return pl.pallas_call(
        functools.partial(_diff_attn_kernel, tq=tq),
        grid=(b, C_HEADS, s // tq),
        in_specs=[pl.BlockSpec((None, tq, 2 * C_QK), lambda bi, h, i: (bi, i, h)),
                  pl.BlockSpec((None, s, 2 * C_QK), lambda bi, h, i: (bi, 0, h)),
                  pl.BlockSpec((None, s, C_V), lambda bi, h, i: (bi, 0, h)),
                  pl.BlockSpec((8, LANES), lambda bi, h, i: (0, 0))],
        out_specs=pl.BlockSpec((None, tq, C_V), lambda bi, h, i: (bi, i, h)),
        out_shape=jax.ShapeDtypeStruct((b, s, C_WIDTH), jnp.float32),
        scratch_shapes=[pltpu.VMEM((2, tq, 1), jnp.float32), pltpu.VMEM((2, tq, 1), jnp.float32),
                        pltpu.VMEM((2, tq, C_V), jnp.float32)],
        compiler_params=_params(3),
        name="diff_attn",
    )(q, k, v, consts)


def _dsa_kernel(qb_ref, qi_ref, w_ref, ki_ref, kb_ref, vb_ref, o_ref,
                keys_sc, qz_sc, m_sc, l_sc, acc_sc, *, tq, tk, n_sel):
    i = pl.program_id(1)
    n_tiles = (i * tq) // tk + 1
    f32 = jnp.float32

    lane = lax.broadcasted_iota(jnp.int32, (tq, LANES), 1)
    for h in range(IDX_HEADS):
        pair = qi_ref[:, (h // 2) * LANES:(h // 2 + 1) * LANES]
        keep = (lane < IDX_DIM) if h % 2 == 0 else (lane >= IDX_DIM)
        qz_sc[h * tq:(h + 1) * tq, :] = jnp.where(keep, pair, jnp.zeros_like(pair))
    wv = w_ref[...]

    def score_tile(j, masked):
        rows = pl.ds(pl.multiple_of(j * tk, tk), tk)
        rel = _dot_nt(qz_sc[...], ki_ref[rows, :])
        sc = jnp.zeros((tq, tk), f32)
        for h in range(IDX_HEADS):
            sc = sc + jnp.maximum(rel[h * tq:(h + 1) * tq], 0.0) * wv[:, h:h + 1]
        if masked:
            sc = jnp.where(_diag_mask(tq, tk, i * tq, j * tk), sc, NEG_INF)
        bits = lax.bitcast_convert_type(sc, jnp.int32)
        key = jnp.where(bits < 0, INT_MIN - bits, bits)
        keys_sc[j] = jnp.where(sc > 0.5 * NEG_INF, key, INT_MIN)

    def score_body(j, carry):
        score_tile(j, False)
        return carry

    lax.fori_loop(0, n_tiles - 1, score_body, 0)
    score_tile(n_tiles - 1, True)

    def count_ge(t):
        def body(j, acc):
            c = jnp.where(keys_sc[j] >= t, 1.0, 0.0)
            for ch in range(tk // LANES):
                acc = acc + c[:, ch * LANES:(ch + 1) * LANES]
            return acc
        acc = lax.fori_loop(0, n_tiles, body, jnp.zeros((tq, LANES), f32))
        return jnp.sum(acc, axis=-1, keepdims=True)

    k_f = float(n_sel)
    t0 = jnp.where(count_ge(jnp.zeros((tq, 1), jnp.int32)) >= k_f, 0, INT_MIN).astype(jnp.int32)

    def bisect(it, t):
        cand = t + lax.shift_left(jnp.int32(1), jnp.int32(30) - it)
        return jnp.where(count_ge(cand) >= k_f, cand, t)

    thr = jnp.maximum(lax.fori_loop(0, 31, bisect, t0), INT_MIN + 1)
    cnt_ge = count_ge(thr)

    @pl.when(jnp.max(cnt_ge) > k_f)
    def _():
        keep_n = k_f - count_ge(thr + 1)
        tri = (lax.broadcasted_iota(jnp.int32, (LANES, LANES), 0)
               <= lax.broadcasted_iota(jnp.int32, (LANES, LANES), 1)).astype(MXU_DTYPE)

        def body(j, carry):
            for ch in range(tk // LANES):
                sl = slice(ch * LANES, (ch + 1) * LANES)
                kc = keys_sc[j, :, sl]
                eq = kc == thr
                rank = carry + _dot(jnp.where(eq, 1.0, 0.0).astype(MXU_DTYPE), tri)
                keys_sc[j, :, sl] = jnp.where(eq & (rank > keep_n), INT_MIN, kc)
                carry = rank[:, LANES - 1:LANES]
            return carry

        lax.fori_loop(0, n_tiles, body, jnp.zeros((tq, 1), f32))

    m_sc[...] = jnp.full(m_sc.shape, NEG_INF, f32)
    l_sc[...] = jnp.zeros(l_sc.shape, f32)
    acc_sc[...] = jnp.zeros(acc_sc.shape, f32)

    def attn_body(j, carry):
        rows = pl.ds(pl.multiple_of(j * tk, tk), tk)
        sel = keys_sc[j] >= thr
        for h in range(B_HEADS):
            sl = slice(h * HEAD_DIM, (h + 1) * HEAD_DIM)
            s = jnp.where(sel, _dot_nt(qb_ref[:, sl], kb_ref[rows, sl]), NEG_INF)
            _online_softmax_step(s, vb_ref[rows, sl], m_sc.at[h], l_sc.at[h], acc_sc.at[h])
        return carry

    lax.fori_loop(0, n_tiles, attn_body, 0)
    for h in range(B_HEADS):
        o_ref[:, h * HEAD_DIM:(h + 1) * HEAD_DIM] = acc_sc[h] / l_sc[h]


def _dsa_attn(qb, qi, w, ki, kb, vb, n_sel):
    b, s, _ = qb.shape
    tq = min(s, 256)
    tk = min(s, 512)
    blk = lambda width: pl.BlockSpec((None, tq, width), lambda bi, i: (bi, i, 0))
    res = lambda width: pl.BlockSpec((None, s, width), lambda bi, i: (bi, 0, 0))
    return pl.pallas_call(
        functools.partial(_dsa_kernel, tq=tq, tk=tk, n_sel=n_sel),
        grid=(b, s // tq),
        in_specs=[blk(B_WIDTH), blk(IDX_HEADS * IDX_DIM), blk(LANES), res(LANES), res(B_WIDTH), res(B_WIDTH)],
        out_specs=blk(B_WIDTH),
        out_shape=jax.ShapeDtypeStruct((b, s, B_WIDTH), jnp.float32),
        scratch_shapes=[pltpu.VMEM((s // tk, tq, tk), jnp.int32),
                        pltpu.VMEM((IDX_HEADS * tq, LANES), MXU_DTYPE),
                        pltpu.VMEM((B_HEADS, tq, 1), jnp.float32),
                        pltpu.VMEM((B_HEADS, tq, 1), jnp.float32),
                        pltpu.VMEM((B_HEADS, tq, HEAD_DIM), jnp.float32)],
        compiler_params=_params(2),
        name="dsa_attn",
    )(qb, qi, w, ki, kb, vb)


def _out_kernel(oa_ref, ob_ref, oc_ref, g_ref, h_ref, p_ref, wo_ref, wpg_ref, wple_ref, gn_ref,
                *out_refs, emit_h):
    g = g_ref[...]
    sg = g * _sigmoid(g)
    h1 = h_ref[...]
    col = 0
    for o_ref in (oa_ref, ob_ref, oc_ref):
        width = o_ref.shape[1]
        mixed = (o_ref[...] * sg[:, col:col + width]).astype(MXU_DTYPE)
        h1 = h1 + _dot(mixed, wo_ref[col:col + width, :])
        col += width
    gate = _sigmoid(_dot(h1.astype(MXU_DTYPE), wpg_ref[...]))
    h2 = h1 + _dot(p_ref[...].astype(MXU_DTYPE), wple_ref[...]) * gate
    if emit_h:
        out_refs[0][...] = h2
    out_refs[-1][...] = _rms(h2, gn_ref[...]).astype(out_refs[-1].dtype)


def _out_proj(oa, ob, oc, g, h, p, wo, wpg, wple, gn, emit_h):
    n, d = h.shape
    tm = min(n, 256)
    row = lambda a: pl.BlockSpec((tm, a.shape[1]), lambda i: (i, 0))
    const = lambda a: pl.BlockSpec(a.shape, lambda i: (0,) * a.ndim, pipeline_mode=pl.Buffered(1))
    out_specs = [pl.BlockSpec((tm, d), lambda i: (i, 0))]
    out_shape = [jax.ShapeDtypeStruct((n, d), MXU_DTYPE if emit_h else jnp.float32)]
    if emit_h:
        out_specs = [pl.BlockSpec((tm, d), lambda i: (i, 0))] + out_specs
        out_shape = [jax.ShapeDtypeStruct((n, d), jnp.float32)] + out_shape
    return pl.pallas_call(
        functools.partial(_out_kernel, emit_h=emit_h),
        grid=(n // tm,),
        in_specs=[row(oa), row(ob), row(oc), row(g), row(h), row(p), const(wo), const(wpg), const(wple),
                  pl.BlockSpec((1, d), lambda i: (0, 0))],
        out_specs=out_specs,
        out_shape=out_shape,
        compiler_params=_params(1),
        name="out_proj",
    )(oa, ob, oc, g, h, p, wo, wpg, wple, gn)


def _in_cols(name):
    k = IN_NAMES.index(name)
    start = sum(IN_SPLITS[:k])
    return start, start + IN_SPLITS[k]


def _gather_cols(w_in, parts):
    cols = []
    for part in parts:
        if isinstance(part, int):
            cols.append(jnp.zeros(w_in.shape[:2] + (part,), w_in.dtype))
        else:
            a, b = _in_cols(part)
            cols.append(w_in[:, :, a:b])
    return jnp.concatenate(cols, axis=-1).astype(MXU_DTYPE)


_SEG_G = ((0, D_MIX, None, None),)
_SEG_B = ((0, B_WIDTH, PAT_DSA, HEAD_DIM ** -0.5), (B_WIDTH, B_WIDTH, PAT_DSA, None),
          (2 * B_WIDTH, B_WIDTH, None, None), (3 * B_WIDTH, IDX_HEADS * IDX_DIM, PAT_SMALL, None),
          (3 * B_WIDTH + IDX_HEADS * IDX_DIM, LANES, PAT_SMALL, None),
          (3 * B_WIDTH + IDX_HEADS * IDX_DIM + LANES, LANES, None, None))
_SEG_C = ((0, C_WIDTH, PAT_SMALL, C_QK ** -0.5), (C_WIDTH, C_WIDTH, PAT_SMALL, None),
          (2 * C_WIDTH, C_WIDTH, None, None))


def kernel(x, p, positions, w_in, w_uq, w_ukv, w_o, norm_g, q_norm_g, kv_norm_g,
           lam_q1, lam_k1, lam_q2, lam_k2, subln_g, w_ple, w_pg, final_g):
    b, s, d = x.shape
    n = b * s
    f32 = jnp.float32
    n_sel = min(TOPK_MAX, s // 4)

    w_a = _gather_cols(w_in, ("c_q", "c_kv", "k_rope", LANES - A_ROPE))
    w_g = _gather_cols(w_in, ("g_a", "g_b", "g_c"))
    w_b = _gather_cols(w_in, ("q_b", "k_b", "v_b", "q_idx", "k_idx", "k_idx", "w_idx", LANES - IDX_HEADS))
    w_c = _gather_cols(w_in, ("q_c", "k_c", "v_c"))
    uq = w_uq.reshape(DEPTH, A_Q_RANK, A_HEADS, A_NOPE + A_ROPE)
    uq = jnp.pad(uq, ((0, 0), (0, 0), (0, 0), (0, A_QK_PAD - A_NOPE - A_ROPE)))
    uq = uq.reshape(DEPTH, A_Q_RANK, A_HEADS * A_QK_PAD).astype(MXU_DTYPE)
    ukv = w_ukv.reshape(DEPTH, A_KV_RANK, A_HEADS, A_NOPE + A_V)
    ukv = jnp.concatenate([ukv[..., :A_NOPE].reshape(DEPTH, A_KV_RANK, A_WIDTH),
                           ukv[..., A_NOPE:].reshape(DEPTH, A_KV_RANK, A_WIDTH)], axis=-1).astype(MXU_DTYPE)
    w_o_c = w_o.astype(MXU_DTYPE)
    w_pg_c = w_pg.astype(MXU_DTYPE)
    w_ple_c = w_ple.astype(MXU_DTYPE)
    pad_lanes = lambda v: jnp.pad(v, (0, LANES - v.shape[0]))

    tabs = _rope_tables(positions.astype(f32).reshape(n, 1))
    h = x.reshape(n, d)
    u = _norm(h, norm_g[0], MXU_DTYPE)
    out = None
    for i in range(DEPTH):
        q_a, k_a, v_a = _mla_proj(u, w_a[i], q_norm_g[i].reshape(1, -1), kv_norm_g[i].reshape(1, -1),
                                  uq[i], ukv[i], tabs)
        (g,) = _proj(u, w_g[i], None, _SEG_G, (f32,), "proj_gates")
        q_b, k_b, v_b, q_i, k_i, w_i = _proj(u, w_b[i], tabs, _SEG_B,
                                             (MXU_DTYPE,) * 5 + (f32,), "proj_dsa")
        q_c, k_c, v_c = _proj(u, w_c[i], tabs, _SEG_C, (MXU_DTYPE,) * 3, "proj_diff")

        r3 = lambda a: a.reshape(b, s, a.shape[-1])
        o_a = _mla_attn(r3(q_a), r3(k_a), r3(v_a))
        o_b = _dsa_attn(r3(q_b), r3(q_i), r3(w_i), r3(k_i), r3(k_b), r3(v_b), n_sel)
        lam_init = 0.8 - 0.6 * math.exp(-0.3 * i)
        consts = jnp.stack([pad_lanes(lam_q1[i]), pad_lanes(lam_k1[i]), pad_lanes(lam_q2[i]),
                            pad_lanes(lam_k2[i]), jnp.full((LANES,), lam_init, f32), subln_g[i],
                            jnp.zeros((LANES,), f32), jnp.zeros((LANES,), f32)]).astype(f32)
        o_c = _diff_attn(r3(q_c), r3(k_c), r3(v_c), consts)

        last = i == DEPTH - 1
        gn = (final_g if last else norm_g[i + 1]).reshape(1, d)
        res = _out_proj(o_a.reshape(n, -1), o_b.reshape(n, -1), o_c.reshape(n, -1), g, h,
                        p[i].reshape(n, PLE_DIM), w_o_c[i], w_pg_c[i], w_ple_c[i], gn, emit_h=not last)
        if last:
            out = res[0]
        else:
            h, u = res
    return out.reshape(b, s, d)
```

```python
import functools
import math

import jax
import jax.numpy as jnp
import numpy as np
from jax import lax
from jax.experimental import pallas as pl
from jax.experimental.pallas import tpu as pltpu

D_MODEL = 2048
DEPTH = 4
CHUNK = 64
ROPE_THETA = 500000.0
NORM_EPS = 1e-6
NEG_INF = -1e30
PLE_DIM = 256
HEAD_DIM = 128

A_HEADS = 6
A_Q_RANK = 384
A_KV_RANK = 256
A_NOPE = 128
A_ROPE = 64
A_V = 128
A_WIDTH = A_HEADS * A_V
B_HEADS = 5
B_WIDTH = B_HEADS * HEAD_DIM
IDX_HEADS = 8
IDX_DIM = 64
TOPK_MAX = 256
C_HEADS = 5
C_QK = 64
C_V = 128
C_WIDTH = C_HEADS * C_V
D_MIX = A_WIDTH + B_WIDTH + C_WIDTH

IN_NAMES = ("c_q", "c_kv", "k_rope", "g_a", "q_b", "k_b", "v_b", "q_idx", "k_idx", "w_idx", "g_b",
            "q_c", "k_c", "v_c", "g_c")
IN_SPLITS = (A_Q_RANK, A_KV_RANK, A_ROPE, A_WIDTH,
             B_WIDTH, B_WIDTH, B_WIDTH, IDX_HEADS * IDX_DIM, IDX_DIM, IDX_HEADS, B_WIDTH,
             C_HEADS * 2 * C_QK, C_HEADS * 2 * C_QK, C_WIDTH, C_WIDTH)

LANES = 128
A_QK_PAD = 256
INT_MIN = -2 ** 31
LOG2E = math.log2(math.e)
MXU_DTYPE = jnp.bfloat16
VMEM_LIMIT = 56 * 1024 * 1024

ROPE_PATTERNS = ((A_ROPE, 64), (HEAD_DIM // 4, 128), (IDX_DIM // 4, 64))
PAT_MLA, PAT_DSA, PAT_SMALL = 0, 1, 2

_NT = (((1,), (1,)), ((), ()))


def _params(n_axes):
    return pltpu.CompilerParams(dimension_semantics=("arbitrary",) * n_axes,
                                vmem_limit_bytes=VMEM_LIMIT)


def _dot(a, b):
    return jnp.dot(a, b, preferred_element_type=jnp.float32)


def _dot_nt(a, b):
    return lax.dot_general(a, b, _NT, preferred_element_type=jnp.float32)


def _sigmoid(x):
    return 1.0 / (1.0 + jnp.exp(-x))


def _rms(x, g):
    return x * lax.rsqrt(jnp.mean(x * x, axis=-1, keepdims=True) + NORM_EPS) * g


def _rope_pattern_rows():
    freq = []
    m_up = np.zeros((3, LANES), np.float32)
    m_dn = np.zeros((3, LANES), np.float32)
    for p, (n_rot, period) in enumerate(ROPE_PATTERNS):
        half = n_rot // 2
        inv = 1.0 / (ROPE_THETA ** (jnp.arange(half, dtype=jnp.float32) * (2.0 / n_rot)))
        one = jnp.concatenate([inv, inv, jnp.zeros((period - n_rot,), jnp.float32)])
        freq.append(jnp.tile(one, LANES // period))
        for lane in range(LANES):
            m = lane % period
            m_up[p, lane] = float(m < half)
            m_dn[p, lane] = float(half <= m < n_rot)
    return jnp.stack(freq), jnp.asarray(m_up), jnp.asarray(m_dn)


def _rope_tab_kernel(pos_ref, f_ref, mu_ref, md_ref, c_ref, sa_ref, sb_ref):
    pos = pos_ref[...]
    for p in range(3):
        ang = pos * f_ref[p:p + 1, :]
        sin = jnp.sin(ang)
        c_ref[p] = jnp.cos(ang)
        sa_ref[p] = -sin * mu_ref[p:p + 1, :]
        sb_ref[p] = sin * md_ref[p:p + 1, :]


def _rope_tables(pos_f32):
    n = pos_f32.shape[0]
    tm = min(n, 1024)
    freq, m_up, m_dn = _rope_pattern_rows()
    row = pl.BlockSpec((3, LANES), lambda i: (0, 0))
    tab = pl.BlockSpec((3, tm, LANES), lambda i: (0, i, 0))
    shape = jax.ShapeDtypeStruct((3, n, LANES), jnp.float32)
    return pl.pallas_call(
        _rope_tab_kernel,
        grid=(n // tm,),
        in_specs=[pl.BlockSpec((tm, 1), lambda i: (i, 0)), row, row, row],
        out_specs=[tab, tab, tab],
        out_shape=[shape, shape, shape],
        compiler_params=_params(1),
        name="rope_tables",
    )(pos_f32, freq, m_up, m_dn)


def _rope128(x, c, sa, sb, half):
    return x * c + pltpu.roll(x, LANES - half, 1) * sa + pltpu.roll(x, half, 1) * sb


def _norm_kernel(x_ref, g_ref, o_ref):
    o_ref[...] = _rms(x_ref[...], g_ref[...]).astype(o_ref.dtype)


def _norm(x, g, out_dtype):
    n, d = x.shape
    tm = min(n, 512)
    return pl.pallas_call(
        _norm_kernel,
        grid=(n // tm,),
        in_specs=[pl.BlockSpec((tm, d), lambda i: (i, 0)), pl.BlockSpec((1, d), lambda i: (0, 0))],
        out_specs=pl.BlockSpec((tm, d), lambda i: (i, 0)),
        out_shape=jax.ShapeDtypeStruct((n, d), out_dtype),
        compiler_params=_params(1),
        name="rms_norm",
    )(x, g.reshape(1, d))


def _proj_kernel(*refs, segs, with_tables):
    u_ref, w_ref = refs[0], refs[1]
    if with_tables:
        c_ref, sa_ref, sb_ref = refs[2:5]
        outs = refs[5:]
    else:
        outs = refs[2:]
    u = u_ref[...]
    for (c0, width, pat, scale), o_ref in zip(segs, outs):
        z = _dot(u, w_ref[:, c0:c0 + width])
        if pat is None and scale is None:
            o_ref[...] = z.astype(o_ref.dtype)
            continue
        for ch in range(width // LANES):
            sl = slice(ch * LANES, (ch + 1) * LANES)
            x = z[:, sl]
            if pat is not None:
                x = _rope128(x, c_ref[pat], sa_ref[pat], sb_ref[pat], ROPE_PATTERNS[pat][0] // 2)
            if scale is not None:
                x = x * scale
            o_ref[:, sl] = x.astype(o_ref.dtype)


def _proj(u, w, tabs, segs, out_dtypes, name):
    n, d = u.shape
    tm = min(n, 512)
    width = w.shape[1]
    in_specs = [pl.BlockSpec((tm, d), lambda i: (i, 0)), pl.BlockSpec((d, width), lambda i: (0, 0))]
    args = [u, w]
    if tabs is not None:
        in_specs += [pl.BlockSpec((3, tm, LANES), lambda i: (0, i, 0))] * 3
        args += list(tabs)
    out_specs = [pl.BlockSpec((tm, s[1]), lambda i: (i, 0)) for s in segs]
    out_shape = [jax.ShapeDtypeStruct((n, s[1]), dt) for s, dt in zip(segs, out_dtypes)]
    return pl.pallas_call(
        functools.partial(_proj_kernel, segs=tuple(segs), with_tables=tabs is not None),
        grid=(n // tm,),
        in_specs=in_specs,
        out_specs=out_specs,
        out_shape=out_shape,
        compiler_params=_params(1),
        name=name,
    )(*args)


def _mla_proj_kernel(u_ref, w_ref, qg_ref, kvg_ref, wuq_ref, wukv_ref, c_ref, sa_ref, sb_ref,
                     q_ref, k_ref, v_ref):
    z = _dot(u_ref[...], w_ref[...])
    c, sa, sb = c_ref[PAT_MLA], sa_ref[PAT_MLA], sb_ref[PAT_MLA]
    half = A_ROPE // 2
    scale = (A_NOPE + A_ROPE) ** -0.5 * LOG2E
    c_q = z[:, :A_Q_RANK]
    c_kv = z[:, A_Q_RANK:A_Q_RANK + A_KV_RANK]
    k_rope = z[:, A_Q_RANK + A_KV_RANK:]
    q = _dot(_rms(c_q, qg_ref[...]).astype(MXU_DTYPE), wuq_ref[...])
    kv = _dot(_rms(c_kv, kvg_ref[...]).astype(MXU_DTYPE), wukv_ref[...])
    k_pe = _rope128(k_rope, c, sa, sb, half).astype(k_ref.dtype)
    for h in range(A_HEADS):
        o = h * A_QK_PAD
        q_ref[:, o:o + A_NOPE] = (q[:, o:o + A_NOPE] * scale).astype(q_ref.dtype)
        q_rot = _rope128(q[:, o + A_NOPE:o + A_QK_PAD], c, sa, sb, half)
        q_ref[:, o + A_NOPE:o + A_QK_PAD] = (q_rot * scale).astype(q_ref.dtype)
        k_ref[:, o:o + A_NOPE] = kv[:, h * A_NOPE:(h + 1) * A_NOPE].astype(k_ref.dtype)
        k_ref[:, o + A_NOPE:o + A_QK_PAD] = k_pe
    v_ref[...] = kv[:, A_WIDTH:].astype(v_ref.dtype)


def _mla_proj(u, w, qg, kvg, wuq, wukv, tabs):
    n, d = u.shape
    tm = min(n, 512)
    full = lambda a: pl.BlockSpec(a.shape, lambda i: (0,) * a.ndim)
    tab = pl.BlockSpec((3, tm, LANES), lambda i: (0, i, 0))
    qk = A_HEADS * A_QK_PAD
    return pl.pallas_call(
        _mla_proj_kernel,
        grid=(n // tm,),
        in_specs=[pl.BlockSpec((tm, d), lambda i: (i, 0)), full(w), full(qg), full(kvg), full(wuq),
                  full(wukv), tab, tab, tab],
        out_specs=[pl.BlockSpec((tm, qk), lambda i: (i, 0)), pl.BlockSpec((tm, qk), lambda i: (i, 0)),
                   pl.BlockSpec((tm, A_WIDTH), lambda i: (i, 0))],
        out_shape=[jax.ShapeDtypeStruct((n, qk), MXU_DTYPE), jax.ShapeDtypeStruct((n, qk), MXU_DTYPE),
                   jax.ShapeDtypeStruct((n, A_WIDTH), MXU_DTYPE)],
        compiler_params=_params(1),
        name="mla_proj",
    )(u, w, qg, kvg, wuq, wukv, *tabs)


def _diag_mask(tq, tk, q0, k0):
    qc = (q0 + lax.broadcasted_iota(jnp.int32, (tq, tk), 0)) // CHUNK
    kc = (k0 + lax.broadcasted_iota(jnp.int32, (tq, tk), 1)) // CHUNK
    return kc <= qc


def _softmax_init(m_ref, l_ref, acc_ref):
    m_ref[...] = jnp.full(m_ref.shape, NEG_INF, jnp.float32)
    l_ref[...] = jnp.zeros(l_ref.shape, jnp.float32)
    acc_ref[...] = jnp.zeros(acc_ref.shape, jnp.float32)


def _online_softmax_step(s, v, m_ref, l_ref, acc_ref):
    chunks = [s[:, c * LANES:(c + 1) * LANES] for c in range(s.shape[1] // LANES)]
    cmax = functools.reduce(jnp.maximum, chunks)
    m_prev = m_ref[...]
    m_new = jnp.maximum(m_prev, jnp.max(cmax, axis=-1, keepdims=True))
    alpha = jnp.exp2(m_prev - m_new)
    ps = [jnp.exp2(c - m_new) for c in chunks]
    l_ref[...] = alpha * l_ref[...] + functools.reduce(jnp.add, ps)
    p = jnp.concatenate([x.astype(v.dtype) for x in ps], axis=1)
    acc_ref[...] = alpha * acc_ref[...] + _dot(p, v)
    m_ref[...] = m_new


def _softmax_finish(l_ref, acc_ref):
    return acc_ref[...] / jnp.sum(l_ref[...], axis=-1, keepdims=True)


MLA_HEADS_PER_STEP = 2


def _mla_attn_kernel(q_ref, k_ref, v_ref, o_ref, m_sc, l_sc, acc_sc, *, tq):
    i = pl.program_id(2)
    _softmax_init(m_sc, l_sc, acc_sc)

    def tile(j, masked):
        rows = pl.ds(pl.multiple_of(j * tq, tq), tq)
        for h in range(MLA_HEADS_PER_STEP):
            qk = slice(h * A_QK_PAD, (h + 1) * A_QK_PAD)
            s = _dot_nt(q_ref[:, qk], k_ref[rows, qk])
            if masked:
                s = jnp.where(_diag_mask(tq, tq, 0, 0), s, NEG_INF)
            _online_softmax_step(s, v_ref[rows, h * A_V:(h + 1) * A_V], m_sc.at[h], l_sc.at[h], acc_sc.at[h])

    def body(j, carry):
        tile(j, False)
        return carry

    lax.fori_loop(0, i, body, 0)
    tile(i, True)
    for h in range(MLA_HEADS_PER_STEP):
        o_ref[:, h * A_V:(h + 1) * A_V] = _softmax_finish(l_sc.at[h], acc_sc.at[h])


def _mla_attn(q, k, v):
    b, s, _ = q.shape
    tq = min(s, 512)
    hp = MLA_HEADS_PER_STEP
    stat = pltpu.VMEM((hp, tq, LANES), jnp.float32)
    return pl.pallas_call(
        functools.partial(_mla_attn_kernel, tq=tq),
        grid=(b, A_HEADS // hp, s // tq),
        in_specs=[pl.BlockSpec((None, tq, hp * A_QK_PAD), lambda bi, h, i: (bi, i, h)),
                  pl.BlockSpec((None, s, hp * A_QK_PAD), lambda bi, h, i: (bi, 0, h)),
                  pl.BlockSpec((None, s, hp * A_V), lambda bi, h, i: (bi, 0, h))],
        out_specs=pl.BlockSpec((None, tq, hp * A_V), lambda bi, h, i: (bi, i, h)),
        out_shape=jax.ShapeDtypeStruct((b, s, A_WIDTH), jnp.float32),
        scratch_shapes=[stat, stat, pltpu.VMEM((hp, tq, A_V), jnp.float32)],
        compiler_params=_params(3),
        name="mla_attn",
    )(q, k, v)


def _diff_attn_kernel(q_ref, k_ref, v_ref, c_ref, o_ref, m_sc, l_sc, acc_sc, *, tq):
    i = pl.program_id(2)
    q = q_ref[...]
    lane = lax.broadcasted_iota(jnp.int32, q.shape, 1)
    zero = jnp.zeros_like(q)
    qs = (jnp.where(lane < C_QK, q, zero), jnp.where(lane >= C_QK, q, zero))
    _softmax_init(m_sc, l_sc, acc_sc)

    def tile(j, masked):
        rows = pl.ds(pl.multiple_of(j * tq, tq), tq)
        k = k_ref[rows, :]
        v = v_ref[rows, :]
        for t in range(2):
            s = _dot_nt(qs[t], k)
            if masked:
                s = jnp.where(_diag_mask(tq, tq, 0, 0), s, NEG_INF)
            _online_softmax_step(s, v, m_sc.at[t], l_sc.at[t], acc_sc.at[t])

    def body(j, carry):
        tile(j, False)
        return carry

    lax.fori_loop(0, i, body, 0)
    tile(i, True)

    c = c_ref[...]
    lam_init = c[4:5, 0:1]
    lam = (jnp.exp(jnp.sum(c[0:1] * c[1:2], axis=-1, keepdims=True))
           - jnp.exp(jnp.sum(c[2:3] * c[3:4], axis=-1, keepdims=True)) + lam_init)
    o = _softmax_finish(l_sc.at[0], acc_sc.at[0]) - lam * _softmax_finish(l_sc.at[1], acc_sc.at[1])
    o_ref[...] = _rms(o, c[5:6]) * (1.0 - lam_init)


def _diff_attn(q, k, v, consts):
    b, s, _ = q.shape
    tq = min(s, 512)
    return pl.pallas_call(
        functools.partial(_diff_attn_kernel, tq=tq),
        grid=(b, C_HEADS, s // tq),
        in_specs=[pl.BlockSpec((None, tq, 2 * C_QK), lambda bi, h, i: (bi, i, h)),
                  pl.BlockSpec((None, s, 2 * C_QK), lambda bi, h, i: (bi, 0, h)),
                  pl.BlockSpec((None, s, C_V), lambda bi, h, i: (bi, 0, h)),
                  pl.BlockSpec((8, LANES), lambda bi, h, i: (0, 0))],
        out_specs=pl.BlockSpec((None, tq, C_V), lambda bi, h, i: (bi, i, h)),
        out_shape=jax.ShapeDtypeStruct((b, s, C_WIDTH), jnp.float32),
        scratch_shapes=[pltpu.VMEM((2, tq, LANES), jnp.float32), pltpu.VMEM((2, tq, LANES), jnp.float32),
                        pltpu.VMEM((2, tq, C_V), jnp.float32)],
        compiler_params=_params(3),
        name="diff_attn",
    )(q, k, v, consts)


def _dsa_kernel(qb_ref, qi_ref, w_ref, ki_ref, kb_ref, vb_ref, o_ref,
                keys_sc, qz_sc, m_sc, l_sc, acc_sc, *, tq, tk, n_sel):
    i = pl.program_id(1)
    n_tiles = (i * tq) // tk + 1
    f32 = jnp.float32

    lane = lax.broadcasted_iota(jnp.int32, (tq, LANES), 1)
    for h in range(IDX_HEADS):
        pair = qi_ref[:, (h // 2) * LANES:(h // 2 + 1) * LANES]
        keep = (lane < IDX_DIM) if h % 2 == 0 else (lane >= IDX_DIM)
        qz_sc[h * tq:(h + 1) * tq, :] = jnp.where(keep, pair, jnp.zeros_like(pair))
    wv = w_ref[...]

    def score_tile(j, masked):
        rows = pl.ds(pl.multiple_of(j * tk, tk), tk)
        rel = _dot_nt(qz_sc[...], ki_ref[rows, :])
        sc = jnp.zeros((tq, tk), f32)
        for h in range(IDX_HEADS):
            sc = sc + jnp.maximum(rel[h * tq:(h + 1) * tq], 0.0) * wv[:, h:h + 1]
        if masked:
            sc = jnp.where(_diag_mask(tq, tk, i * tq, j * tk), sc, NEG_INF)
        bits = lax.bitcast_convert_type(sc, jnp.int32)
        key = jnp.where(bits < 0, INT_MIN - bits, bits)
        keys_sc[j] = jnp.where(sc > 0.5 * NEG_INF, key, INT_MIN)

    def score_body(j, carry):
        score_tile(j, False)
        return carry

    lax.fori_loop(0, n_tiles - 1, score_body, 0)
    score_tile(n_tiles - 1, True)

    def count_ge(t):
        def body(j, acc):
            for ch in range(tk // LANES):
                acc = acc + jnp.where(keys_sc[j, :, ch * LANES:(ch + 1) * LANES] >= t, 1.0, 0.0)
            return acc
        acc = lax.fori_loop(0, n_tiles, body, jnp.zeros((tq, LANES), f32))
        return jnp.broadcast_to(jnp.sum(acc, axis=-1, keepdims=True), (tq, LANES))

    k_f = float(n_sel)
    t0 = jnp.where(count_ge(jnp.zeros((tq, LANES), jnp.int32)) >= k_f, 0, INT_MIN).astype(jnp.int32)

    def bisect(it, t):
        cand = t + lax.shift_left(jnp.int32(1), jnp.int32(30) - it)
        return jnp.where(count_ge(cand) >= k_f, cand, t)

    thr = jnp.maximum(lax.fori_loop(0, 31, bisect, t0), INT_MIN + 1)
    cnt_ge = count_ge(thr)

    @pl.when(jnp.max(cnt_ge) > k_f)
    def _():
        keep_n = k_f - count_ge(thr + 1)
        tri = (lax.broadcasted_iota(jnp.int32, (LANES, LANES), 0)
               <= lax.broadcasted_iota(jnp.int32, (LANES, LANES), 1)).astype(MXU_DTYPE)

        def body(j, carry):
            for ch in range(tk // LANES):
                sl = slice(ch * LANES, (ch + 1) * LANES)
                kc = keys_sc[j, :, sl]
                eq = kc == thr
                rank = carry + _dot(jnp.where(eq, 1.0, 0.0).astype(MXU_DTYPE), tri)
                keys_sc[j, :, sl] = jnp.where(eq & (rank > keep_n), INT_MIN, kc)
                carry = jnp.broadcast_to(rank[:, LANES - 1:LANES], (tq, LANES))
            return carry

        lax.fori_loop(0, n_tiles, body, jnp.zeros((tq, LANES), f32))

    _softmax_init(m_sc, l_sc, acc_sc)

    def attn_body(j, carry):
        rows = pl.ds(pl.multiple_of(j * tk, tk), tk)
        for h in range(B_HEADS):
            sl = slice(h * HEAD_DIM, (h + 1) * HEAD_DIM)
            s = _dot_nt(qb_ref[:, sl], kb_ref[rows, sl])
            s = jnp.concatenate(
                [jnp.where(keys_sc[j, :, c * LANES:(c + 1) * LANES] >= thr, s[:, c * LANES:(c + 1) * LANES], NEG_INF)
                 for c in range(tk // LANES)], axis=1)
            _online_softmax_step(s, vb_ref[rows, sl], m_sc.at[h], l_sc.at[h], acc_sc.at[h])
        return carry

    lax.fori_loop(0, n_tiles, attn_body, 0)
    for h in range(B_HEADS):
        o_ref[:, h * HEAD_DIM:(h + 1) * HEAD_DIM] = _softmax_finish(l_sc.at[h], acc_sc.at[h])


def _dsa_attn(qb, qi, w, ki, kb, vb, n_sel):
    b, s, _ = qb.shape
    tq = min(s, 256)
    tk = min(s, 512)
    blk = lambda width: pl.BlockSpec((None, tq, width), lambda bi, i: (bi, i, 0))
    res = lambda width: pl.BlockSpec((None, s, width), lambda bi, i: (bi, 0, 0))
    return pl.pallas_call(
        functools.partial(_dsa_kernel, tq=tq, tk=tk, n_sel=n_sel),
        grid=(b, s // tq),
        in_specs=[blk(B_WIDTH), blk(IDX_HEADS * IDX_DIM), blk(LANES), res(LANES), res(B_WIDTH), res(B_WIDTH)],
        out_specs=blk(B_WIDTH),
        out_shape=jax.ShapeDtypeStruct((b, s, B_WIDTH), jnp.float32),
        scratch_shapes=[pltpu.VMEM((s // tk, tq, tk), jnp.int32),
                        pltpu.VMEM((IDX_HEADS * tq, LANES), MXU_DTYPE),
                        pltpu.VMEM((B_HEADS, tq, LANES), jnp.float32),
                        pltpu.VMEM((B_HEADS, tq, LANES), jnp.float32),
                        pltpu.VMEM((B_HEADS, tq, HEAD_DIM), jnp.float32)],
        compiler_params=_params(2),
        name="dsa_attn",
    )(qb, qi, w, ki, kb, vb)


def _out_kernel(oa_ref, ob_ref, oc_ref, g_ref, h_ref, p_ref, wo_ref, wpg_ref, wple_ref, gn_ref,
                *out_refs, emit_h):
    g = g_ref[...]
    sg = g * _sigmoid(g)
    h1 = h_ref[...]
    col = 0
    for o_ref in (oa_ref, ob_ref, oc_ref):
        width = o_ref.shape[1]
        mixed = (o_ref[...] * sg[:, col:col + width]).astype(MXU_DTYPE)
        h1 = h1 + _dot(mixed, wo_ref[col:col + width, :])
        col += width
    gate = _sigmoid(_dot(h1.astype(MXU_DTYPE), wpg_ref[...]))
    h2 = h1 + _dot(p_ref[...].astype(MXU_DTYPE), wple_ref[...]) * gate
    if emit_h:
        out_refs[0][...] = h2
    out_refs[-1][...] = _rms(h2, gn_ref[...]).astype(out_refs[-1].dtype)


def _out_proj(oa, ob, oc, g, h, p, wo, wpg, wple, gn, emit_h):
    n, d = h.shape
    tm = min(n, 256)
    row = lambda a: pl.BlockSpec((tm, a.shape[1]), lambda i: (i, 0))
    const = lambda a: pl.BlockSpec(a.shape, lambda i: (0,) * a.ndim, pipeline_mode=pl.Buffered(1))
    out_specs = [pl.BlockSpec((tm, d), lambda i: (i, 0))]
    out_shape = [jax.ShapeDtypeStruct((n, d), MXU_DTYPE if emit_h else jnp.float32)]
    if emit_h:
        out_specs = [pl.BlockSpec((tm, d), lambda i: (i, 0))] + out_specs
        out_shape = [jax.ShapeDtypeStruct((n, d), jnp.float32)] + out_shape
    return pl.pallas_call(
        functools.partial(_out_kernel, emit_h=emit_h),
        grid=(n // tm,),
        in_specs=[row(oa), row(ob), row(oc), row(g), row(h), row(p), const(wo), const(wpg), const(wple),
                  pl.BlockSpec((1, d), lambda i: (0, 0))],
        out_specs=out_specs,
        out_shape=out_shape,
        compiler_params=_params(1),
        name="out_proj",
    )(oa, ob, oc, g, h, p, wo, wpg, wple, gn)


def _in_cols(name):
    k = IN_NAMES.index(name)
    start = sum(IN_SPLITS[:k])
    return start, start + IN_SPLITS[k]


def _gather_cols(w_in, parts):
    cols = []
    for part in parts:
        if isinstance(part, int):
            cols.append(jnp.zeros(w_in.shape[:2] + (part,), w_in.dtype))
        else:
            a, b = _in_cols(part)
            cols.append(w_in[:, :, a:b])
    return jnp.concatenate(cols, axis=-1).astype(MXU_DTYPE)


_SEG_G = ((0, D_MIX, None, None),)
_SEG_B = ((0, B_WIDTH, PAT_DSA, HEAD_DIM ** -0.5 * LOG2E), (B_WIDTH, B_WIDTH, PAT_DSA, None),
          (2 * B_WIDTH, B_WIDTH, None, None), (3 * B_WIDTH, IDX_HEADS * IDX_DIM, PAT_SMALL, None),
          (3 * B_WIDTH + IDX_HEADS * IDX_DIM, LANES, PAT_SMALL, None),
          (3 * B_WIDTH + IDX_HEADS * IDX_DIM + LANES, LANES, None, None))
_SEG_C = ((0, C_WIDTH, PAT_SMALL, C_QK ** -0.5 * LOG2E), (C_WIDTH, C_WIDTH, PAT_SMALL, None),
          (2 * C_WIDTH, C_WIDTH, None, None))


def kernel(x, p, positions, w_in, w_uq, w_ukv, w_o, norm_g, q_norm_g, kv_norm_g,
           lam_q1, lam_k1, lam_q2, lam_k2, subln_g, w_ple, w_pg, final_g):
    b, s, d = x.shape
    n = b * s
    f32 = jnp.float32
    n_sel = min(TOPK_MAX, s // 4)

    w_a = _gather_cols(w_in, ("c_q", "c_kv", "k_rope", LANES - A_ROPE))
    w_g = _gather_cols(w_in, ("g_a", "g_b", "g_c"))
    w_b = _gather_cols(w_in, ("q_b", "k_b", "v_b", "q_idx", "k_idx", "k_idx", "w_idx", LANES - IDX_HEADS))
    w_c = _gather_cols(w_in, ("q_c", "k_c", "v_c"))
    uq = w_uq.reshape(DEPTH, A_Q_RANK, A_HEADS, A_NOPE + A_ROPE)
    uq = jnp.pad(uq, ((0, 0), (0, 0), (0, 0), (0, A_QK_PAD - A_NOPE - A_ROPE)))
    uq = uq.reshape(DEPTH, A_Q_RANK, A_HEADS * A_QK_PAD).astype(MXU_DTYPE)
    ukv = w_ukv.reshape(DEPTH, A_KV_RANK, A_HEADS, A_NOPE + A_V)
    ukv = jnp.concatenate([ukv[..., :A_NOPE].reshape(DEPTH, A_KV_RANK, A_WIDTH),
                           ukv[..., A_NOPE:].reshape(DEPTH, A_KV_RANK, A_WIDTH)], axis=-1).astype(MXU_DTYPE)
    w_o_c = w_o.astype(MXU_DTYPE)
    w_pg_c = w_pg.astype(MXU_DTYPE)
    w_ple_c = w_ple.astype(MXU_DTYPE)
    pad_lanes = lambda v: jnp.pad(v, (0, LANES - v.shape[0]))

    tabs = _rope_tables(positions.astype(f32).reshape(n, 1))
    h = x.reshape(n, d)
    u = _norm(h, norm_g[0], MXU_DTYPE)
    out = None
    for i in range(DEPTH):
        q_a, k_a, v_a = _mla_proj(u, w_a[i], q_norm_g[i].reshape(1, -1), kv_norm_g[i].reshape(1, -1),
                                  uq[i], ukv[i], tabs)
        (g,) = _proj(u, w_g[i], None, _SEG_G, (f32,), "proj_gates")
        q_b, k_b, v_b, q_i, k_i, w_i = _proj(u, w_b[i], tabs, _SEG_B,
                                             (MXU_DTYPE,) * 5 + (f32,), "proj_dsa")
        q_c, k_c, v_c = _proj(u, w_c[i], tabs, _SEG_C, (MXU_DTYPE,) * 3, "proj_diff")

        r3 = lambda a: a.reshape(b, s, a.shape[-1])
        o_a = _mla_attn(r3(q_a), r3(k_a), r3(v_a))
        o_b = _dsa_attn(r3(q_b), r3(q_i), r3(w_i), r3(k_i), r3(k_b), r3(v_b), n_sel)
        lam_init = 0.8 - 0.6 * math.exp(-0.3 * i)
        consts = jnp.stack([pad_lanes(lam_q1[i]), pad_lanes(lam_k1[i]), pad_lanes(lam_q2[i]),
                            pad_lanes(lam_k2[i]), jnp.full((LANES,), lam_init, f32), subln_g[i],
                            jnp.zeros((LANES,), f32), jnp.zeros((LANES,), f32)]).astype(f32)
        o_c = _diff_attn(r3(q_c), r3(k_c), r3(v_c), consts)

        last = i == DEPTH - 1
        gn = (final_g if last else norm_g[i + 1]).reshape(1, d)
        res = _out_proj(o_a.reshape(n, -1), o_b.reshape(n, -1), o_c.reshape(n, -1), g, h,
                        p[i].reshape(n, PLE_DIM), w_o_c[i], w_pg_c[i], w_ple_c[i], gn, emit_h=not last)
        if last:
            out = res[0]
        else:
            h, u = res
    return out.reshape(b, s, d)
```

```python
import functools
import math

import jax
import jax.numpy as jnp
import numpy as np
from jax import lax
from jax.experimental import pallas as pl
from jax.experimental.pallas import tpu as pltpu

D_MODEL = 2048
DEPTH = 4
CHUNK = 64
ROPE_THETA = 500000.0
NORM_EPS = 1e-6
NEG_INF = -1e30
PLE_DIM = 256
HEAD_DIM = 128

A_HEADS = 6
A_Q_RANK = 384
A_KV_RANK = 256
A_NOPE = 128
A_ROPE = 64
A_V = 128
A_WIDTH = A_HEADS * A_V
B_HEADS = 5
B_WIDTH = B_HEADS * HEAD_DIM
IDX_HEADS = 8
IDX_DIM = 64
TOPK_MAX = 256
C_HEADS = 5
C_QK = 64
C_V = 128
C_WIDTH = C_HEADS * C_V
D_MIX = A_WIDTH + B_WIDTH + C_WIDTH

IN_NAMES = ("c_q", "c_kv", "k_rope", "g_a", "q_b", "k_b", "v_b", "q_idx", "k_idx", "w_idx", "g_b",
            "q_c", "k_c", "v_c", "g_c")
IN_SPLITS = (A_Q_RANK, A_KV_RANK, A_ROPE, A_WIDTH,
             B_WIDTH, B_WIDTH, B_WIDTH, IDX_HEADS * IDX_DIM, IDX_DIM, IDX_HEADS, B_WIDTH,
             C_HEADS * 2 * C_QK, C_HEADS * 2 * C_QK, C_WIDTH, C_WIDTH)

LANES = 128
A_QK_PAD = 256
INT_MIN = -2 ** 31
LOG2E = math.log2(math.e)
MXU_DTYPE = jnp.bfloat16
VMEM_LIMIT = 56 * 1024 * 1024

ROPE_PATTERNS = ((A_ROPE, 64), (HEAD_DIM // 4, 128), (IDX_DIM // 4, 64))
PAT_MLA, PAT_DSA, PAT_SMALL = 0, 1, 2

_NT = (((1,), (1,)), ((), ()))


def _params(n_axes):
    return pltpu.CompilerParams(dimension_semantics=("arbitrary",) * n_axes,
                                vmem_limit_bytes=VMEM_LIMIT)


def _dot(a, b):
    return jnp.dot(a, b, preferred_element_type=jnp.float32)


def _dot_nt(a, b):
    return lax.dot_general(a, b, _NT, preferred_element_type=jnp.float32)


def _sigmoid(x):
    return 1.0 / (1.0 + jnp.exp(-x))


def _rms(x, g):
    return x * lax.rsqrt(jnp.mean(x * x, axis=-1, keepdims=True) + NORM_EPS) * g


def _rope_pattern_rows():
    freq = []
    m_up = np.zeros((3, LANES), np.float32)
    m_dn = np.zeros((3, LANES), np.float32)
    for p, (n_rot, period) in enumerate(ROPE_PATTERNS):
        half = n_rot // 2
        inv = 1.0 / (ROPE_THETA ** (jnp.arange(half, dtype=jnp.float32) * (2.0 / n_rot)))
        one = jnp.concatenate([inv, inv, jnp.zeros((period - n_rot,), jnp.float32)])
        freq.append(jnp.tile(one, LANES // period))
        for lane in range(LANES):
            m = lane % period
            m_up[p, lane] = float(m < half)
            m_dn[p, lane] = float(half <= m < n_rot)
    return jnp.stack(freq), jnp.asarray(m_up), jnp.asarray(m_dn)


def _rope_tab_kernel(pos_ref, f_ref, mu_ref, md_ref, c_ref, sa_ref, sb_ref):
    pos = pos_ref[...]
    for p in range(3):
        ang = pos * f_ref[p:p + 1, :]
        sin = jnp.sin(ang)
        c_ref[p] = jnp.cos(ang)
        sa_ref[p] = -sin * mu_ref[p:p + 1, :]
        sb_ref[p] = sin * md_ref[p:p + 1, :]


def _rope_tables(pos_f32):
    n = pos_f32.shape[0]
    tm = min(n, 1024)
    freq, m_up, m_dn = _rope_pattern_rows()
    row = pl.BlockSpec((3, LANES), lambda i: (0, 0))
    tab = pl.BlockSpec((3, tm, LANES), lambda i: (0, i, 0))
    shape = jax.ShapeDtypeStruct((3, n, LANES), jnp.float32)
    return pl.pallas_call(
        _rope_tab_kernel,
        grid=(n // tm,),
        in_specs=[pl.BlockSpec((tm, 1), lambda i: (i, 0)), row, row, row],
        out_specs=[tab, tab, tab],
        out_shape=[shape, shape, shape],
        compiler_params=_params(1),
        name="rope_tables",
    )(pos_f32, freq, m_up, m_dn)


def _rope128(x, c, sa, sb, half):
    return x * c + pltpu.roll(x, LANES - half, 1) * sa + pltpu.roll(x, half, 1) * sb


def _norm_kernel(x_ref, g_ref, o_ref):
    o_ref[...] = _rms(x_ref[...], g_ref[...]).astype(o_ref.dtype)


def _norm(x, g, out_dtype):
    n, d = x.shape
    tm = min(n, 512)
    return pl.pallas_call(
        _norm_kernel,
        grid=(n // tm,),
        in_specs=[pl.BlockSpec((tm, d), lambda i: (i, 0)), pl.BlockSpec((1, d), lambda i: (0, 0))],
        out_specs=pl.BlockSpec((tm, d), lambda i: (i, 0)),
        out_shape=jax.ShapeDtypeStruct((n, d), out_dtype),
        compiler_params=_params(1),
        name="rms_norm",
    )(x, g.reshape(1, d))


def _proj_kernel(*refs, segs, with_tables):
    u_ref, w_ref = refs[0], refs[1]
    if with_tables:
        c_ref, sa_ref, sb_ref = refs[2:5]
        outs = refs[5:]
    else:
        outs = refs[2:]
    u = u_ref[...]
    for (c0, width, pat, scale), o_ref in zip(segs, outs):
        z = _dot(u, w_ref[:, c0:c0 + width])
        if pat is None and scale is None:
            o_ref[...] = z.astype(o_ref.dtype)
            continue
        for ch in range(width // LANES):
            sl = slice(ch * LANES, (ch + 1) * LANES)
            x = z[:, sl]
            if pat is not None:
                x = _rope128(x, c_ref[pat], sa_ref[pat], sb_ref[pat], ROPE_PATTERNS[pat][0] // 2)
            if scale is not None:
                x = x * scale
            o_ref[:, sl] = x.astype(o_ref.dtype)


def _proj(u, w, tabs, segs, out_dtypes, name):
    n, d = u.shape
    tm = min(n, 512)
    width = w.shape[1]
    in_specs = [pl.BlockSpec((tm, d), lambda i: (i, 0)), pl.BlockSpec((d, width), lambda i: (0, 0))]
    args = [u, w]
    if tabs is not None:
        in_specs += [pl.BlockSpec((3, tm, LANES), lambda i: (0, i, 0))] * 3
        args += list(tabs)
    out_specs = [pl.BlockSpec((tm, s[1]), lambda i: (i, 0)) for s in segs]
    out_shape = [jax.ShapeDtypeStruct((n, s[1]), dt) for s, dt in zip(segs, out_dtypes)]
    return pl.pallas_call(
        functools.partial(_proj_kernel, segs=tuple(segs), with_tables=tabs is not None),
        grid=(n // tm,),
        in_specs=in_specs,
        out_specs=out_specs,
        out_shape=out_shape,
        compiler_params=_params(1),
        name=name,
    )(*args)


def _mla_proj_kernel(u_ref, w_ref, qg_ref, kvg_ref, wuq_ref, wukv_ref, c_ref, sa_ref, sb_ref,
                     q_ref, k_ref, v_ref):
    z = _dot(u_ref[...], w_ref[...])
    c, sa, sb = c_ref[PAT_MLA], sa_ref[PAT_MLA], sb_ref[PAT_MLA]
    half = A_ROPE // 2
    scale = (A_NOPE + A_ROPE) ** -0.5 * LOG2E
    c_q = z[:, :A_Q_RANK]
    c_kv = z[:, A_Q_RANK:A_Q_RANK + A_KV_RANK]
    k_rope = z[:, A_Q_RANK + A_KV_RANK:]
    q = _dot(_rms(c_q, qg_ref[...]).astype(MXU_DTYPE), wuq_ref[...])
    kv = _dot(_rms(c_kv, kvg_ref[...]).astype(MXU_DTYPE), wukv_ref[...])
    k_pe = _rope128(k_rope, c, sa, sb, half).astype(k_ref.dtype)
    for h in range(A_HEADS):
        o = h * A_QK_PAD
        q_ref[:, o:o + A_NOPE] = (q[:, o:o + A_NOPE] * scale).astype(q_ref.dtype)
        q_rot = _rope128(q[:, o + A_NOPE:o + A_QK_PAD], c, sa, sb, half)
        q_ref[:, o + A_NOPE:o + A_QK_PAD] = (q_rot * scale).astype(q_ref.dtype)
        k_ref[:, o:o + A_NOPE] = kv[:, h * A_NOPE:(h + 1) * A_NOPE].astype(k_ref.dtype)
        k_ref[:, o + A_NOPE:o + A_QK_PAD] = k_pe
    v_ref[...] = kv[:, A_WIDTH:].astype(v_ref.dtype)


def _mla_proj(u, w, qg, kvg, wuq, wukv, tabs):
    n, d = u.shape
    tm = min(n, 512)
    full = lambda a: pl.BlockSpec(a.shape, lambda i: (0,) * a.ndim)
    tab = pl.BlockSpec((3, tm, LANES), lambda i: (0, i, 0))
    qk = A_HEADS * A_QK_PAD
    return pl.pallas_call(
        _mla_proj_kernel,
        grid=(n // tm,),
        in_specs=[pl.BlockSpec((tm, d), lambda i: (i, 0)), full(w), full(qg), full(kvg), full(wuq),
                  full(wukv), tab, tab, tab],
        out_specs=[pl.BlockSpec((tm, qk), lambda i: (i, 0)), pl.BlockSpec((tm, qk), lambda i: (i, 0)),
                   pl.BlockSpec((tm, A_WIDTH), lambda i: (i, 0))],
        out_shape=[jax.ShapeDtypeStruct((n, qk), MXU_DTYPE), jax.ShapeDtypeStruct((n, qk), MXU_DTYPE),
                   jax.ShapeDtypeStruct((n, A_WIDTH), MXU_DTYPE)],
        compiler_params=_params(1),
        name="mla_proj",
    )(u, w, qg, kvg, wuq, wukv, *tabs)


def _diag_mask(tq, tk, q0, k0):
    qc = (q0 + lax.broadcasted_iota(jnp.int32, (tq, tk), 0)) // CHUNK
    kc = (k0 + lax.broadcasted_iota(jnp.int32, (tq, tk), 1)) // CHUNK
    return kc <= qc


def _softmax_init(m_ref, l_ref, acc_ref):
    m_ref[...] = jnp.full(m_ref.shape, NEG_INF, jnp.float32)
    l_ref[...] = jnp.zeros(l_ref.shape, jnp.float32)
    acc_ref[...] = jnp.zeros(acc_ref.shape, jnp.float32)


def _online_softmax_step(s, v, m_ref, l_ref, acc_ref):
    chunks = [s[:, c * LANES:(c + 1) * LANES] for c in range(s.shape[1] // LANES)]
    cmax = functools.reduce(jnp.maximum, chunks)
    m_prev = m_ref[...]
    m_new = jnp.maximum(m_prev, jnp.max(cmax, axis=-1, keepdims=True))
    alpha = jnp.exp2(m_prev - m_new)
    ps = [jnp.exp2(c - m_new) for c in chunks]
    l_ref[...] = alpha * l_ref[...] + functools.reduce(jnp.add, ps)
    p = jnp.concatenate([x.astype(v.dtype) for x in ps], axis=1)
    acc_ref[...] = alpha * acc_ref[...] + _dot(p, v)
    m_ref[...] = m_new


def _softmax_finish(l_ref, acc_ref):
    return acc_ref[...] / jnp.sum(l_ref[...], axis=-1, keepdims=True)


MLA_HEADS_PER_STEP = 3


def _mla_attn_kernel(q_ref, k_ref, v_ref, o_ref, m_sc, l_sc, acc_sc, *, tq):
    i = pl.program_id(2)
    _softmax_init(m_sc, l_sc, acc_sc)

    def tile(j, masked):
        rows = pl.ds(pl.multiple_of(j * tq, tq), tq)
        for h in range(MLA_HEADS_PER_STEP):
            qk = slice(h * A_QK_PAD, (h + 1) * A_QK_PAD)
            s = _dot_nt(q_ref[:, qk], k_ref[rows, qk])
            if masked:
                s = jnp.where(_diag_mask(tq, tq, 0, 0), s, NEG_INF)
            _online_softmax_step(s, v_ref[rows, h * A_V:(h + 1) * A_V], m_sc.at[h], l_sc.at[h], acc_sc.at[h])

    def body(j, carry):
        tile(j, False)
        return carry

    lax.fori_loop(0, i, body, 0)
    tile(i, True)
    for h in range(MLA_HEADS_PER_STEP):
        o_ref[:, h * A_V:(h + 1) * A_V] = _softmax_finish(l_sc.at[h], acc_sc.at[h])


def _mla_attn(q, k, v):
    b, s, _ = q.shape
    tq = min(s, 512)
    hp = MLA_HEADS_PER_STEP
    stat = pltpu.VMEM((hp, tq, LANES), jnp.float32)
    return pl.pallas_call(
        functools.partial(_mla_attn_kernel, tq=tq),
        grid=(b, A_HEADS // hp, s // tq),
        in_specs=[pl.BlockSpec((None, tq, hp * A_QK_PAD), lambda bi, h, i: (bi, i, h)),
                  pl.BlockSpec((None, s, hp * A_QK_PAD), lambda bi, h, i: (bi, 0, h)),
                  pl.BlockSpec((None, s, hp * A_V), lambda bi, h, i: (bi, 0, h))],
        out_specs=pl.BlockSpec((None, tq, hp * A_V), lambda bi, h, i: (bi, i, h)),
        out_shape=jax.ShapeDtypeStruct((b, s, A_WIDTH), jnp.float32),
        scratch_shapes=[stat, stat, pltpu.VMEM((hp, tq, A_V), jnp.float32)],
        compiler_params=_params(3),
        name="mla_attn",
    )(q, k, v)


def _diff_attn_kernel(q_ref, k_ref, v_ref, c_ref, o_ref, m_sc, l_sc, acc_sc, *, tq):
    i = pl.program_id(2)
    q = q_ref[...]
    lane = lax.broadcasted_iota(jnp.int32, q.shape, 1)
    zero = jnp.zeros_like(q)
    qs = (jnp.where(lane < C_QK, q, zero), jnp.where(lane >= C_QK, q, zero))
    _softmax_init(m_sc, l_sc, acc_sc)

    def tile(j, masked):
        rows = pl.ds(pl.multiple_of(j * tq, tq), tq)
        k = k_ref[rows, :]
        v = v_ref[rows, :]
        for t in range(2):
            s = _dot_nt(qs[t], k)
            if masked:
                s = jnp.where(_diag_mask(tq, tq, 0, 0), s, NEG_INF)
            _online_softmax_step(s, v, m_sc.at[t], l_sc.at[t], acc_sc.at[t])

    def body(j, carry):
        tile(j, False)
        return carry

    lax.fori_loop(0, i, body, 0)
    tile(i, True)

    c = c_ref[...]
    lam_init = c[4:5, 0:1]
    lam = (jnp.exp(jnp.sum(c[0:1] * c[1:2], axis=-1, keepdims=True))
           - jnp.exp(jnp.sum(c[2:3] * c[3:4], axis=-1, keepdims=True)) + lam_init)
    o = _softmax_finish(l_sc.at[0], acc_sc.at[0]) - lam * _softmax_finish(l_sc.at[1], acc_sc.at[1])
    o_ref[...] = _rms(o, c[5:6]) * (1.0 - lam_init)


def _diff_attn(q, k, v, consts):
    b, s, _ = q.shape
    tq = min(s, 512)
    return pl.pallas_call(
        functools.partial(_diff_attn_kernel, tq=tq),
        grid=(b, C_HEADS, s // tq),
        in_specs=[pl.BlockSpec((None, tq, 2 * C_QK), lambda bi, h, i: (bi, i, h)),
                  pl.BlockSpec((None, s, 2 * C_QK), lambda bi, h, i: (bi, 0, h)),
                  pl.BlockSpec((None, s, C_V), lambda bi, h, i: (bi, 0, h)),
                  pl.BlockSpec((8, LANES), lambda bi, h, i: (0, 0))],
        out_specs=pl.BlockSpec((None, tq, C_V), lambda bi, h, i: (bi, i, h)),
        out_shape=jax.ShapeDtypeStruct((b, s, C_WIDTH), jnp.float32),
        scratch_shapes=[pltpu.VMEM((2, tq, LANES), jnp.float32), pltpu.VMEM((2, tq, LANES), jnp.float32),
                        pltpu.VMEM((2, tq, C_V), jnp.float32)],
        compiler_params=_params(3),
        name="diff_attn",
    )(q, k, v, consts)


SUBLANES = 8
COUNT_CHAINS = 4


def _tree_sum(parts):
    while len(parts) > 1:
        parts = [a + b for a, b in zip(parts[0::2], parts[1::2])] + parts[len(parts) & ~1:]
    return parts[0]


def _dsa_kernel(qb_ref, qi_ref, w_ref, ki_ref, kb_ref, vb_ref, o_ref,
                keys_sc, bias_sc, qz_sc, m_sc, l_sc, acc_sc, *, tq, tk, n_sel):
    i = pl.program_id(1)
    n_tiles = (i * tq) // tk + 1
    f32 = jnp.float32

    lane = lax.broadcasted_iota(jnp.int32, (tq, LANES), 1)
    for h in range(IDX_HEADS):
        pair = qi_ref[:, (h // 2) * LANES:(h // 2 + 1) * LANES]
        keep = (lane < IDX_DIM) if h % 2 == 0 else (lane >= IDX_DIM)
        qz_sc[h * tq:(h + 1) * tq, :] = jnp.where(keep, pair, jnp.zeros_like(pair))
    w_t = w_ref[...].T

    def score_tile(j, masked):
        rows = pl.ds(pl.multiple_of(j * tk, tk), tk)
        rel = _dot_nt(ki_ref[rows, :], qz_sc[...])
        sc = jnp.zeros((tk, tq), f32)
        for h in range(IDX_HEADS):
            sc = sc + jnp.maximum(rel[:, h * tq:(h + 1) * tq], 0.0) * w_t[h:h + 1, :]
        if masked:
            kc = (j * tk + lax.broadcasted_iota(jnp.int32, (tk, tq), 0)) // CHUNK
            qc = (i * tq + lax.broadcasted_iota(jnp.int32, (tk, tq), 1)) // CHUNK
            sc = jnp.where(kc <= qc, sc, NEG_INF)
        bits = lax.bitcast_convert_type(sc, jnp.int32)
        key = jnp.where(bits < 0, INT_MIN - bits, bits)
        keys_sc[j] = jnp.where(sc > 0.5 * NEG_INF, key, INT_MIN)

    def score_body(j, carry):
        score_tile(j, False)
        return carry

    lax.fori_loop(0, n_tiles - 1, score_body, 0)
    score_tile(n_tiles - 1, True)

    def count_ge(t):
        def body(j, accs):
            accs = list(accs)
            for n, r in enumerate(range(0, tk, SUBLANES)):
                hit = jnp.where(keys_sc[j, r:r + SUBLANES, :] >= t, 1.0, 0.0)
                accs[n % COUNT_CHAINS] = accs[n % COUNT_CHAINS] + hit
            return tuple(accs)
        zero = jnp.zeros((SUBLANES, tq), f32)
        accs = lax.fori_loop(0, n_tiles, body, (zero,) * COUNT_CHAINS)
        return jnp.sum(_tree_sum(list(accs)), axis=0, keepdims=True)

    k_f = float(n_sel)
    t0 = jnp.where(count_ge(jnp.zeros((1, tq), jnp.int32)) >= k_f, 0, INT_MIN).astype(jnp.int32)

    def bisect(it, t):
        cand = t + lax.shift_left(jnp.int32(1), jnp.int32(30) - it)
        return jnp.where(count_ge(cand) >= k_f, cand, t)

    thr = jnp.maximum(lax.fori_loop(0, 31, bisect, t0), INT_MIN + 1)
    cnt_ge = count_ge(thr)

    @pl.when(jnp.max(cnt_ge) > k_f)
    def _():
        keep_n = k_f - count_ge(thr + 1)
        tri = (lax.broadcasted_iota(jnp.int32, (tk, tk), 0)
               >= lax.broadcasted_iota(jnp.int32, (tk, tk), 1)).astype(MXU_DTYPE)

        def body(j, carry):
            kt = keys_sc[j]
            eq = kt == thr
            rank = carry + _dot(tri, jnp.where(eq, 1.0, 0.0).astype(MXU_DTYPE))
            keys_sc[j] = jnp.where(eq & (rank > keep_n), INT_MIN, kt)
            return rank[tk - 1:tk, :]

        lax.fori_loop(0, n_tiles, body, jnp.zeros((1, tq), f32))

    def bias_body(j, carry):
        bias_sc[j] = jnp.where(keys_sc[j] >= thr, 0.0, NEG_INF).T
        return carry

    lax.fori_loop(0, n_tiles, bias_body, 0)
    _softmax_init(m_sc, l_sc, acc_sc)

    def attn_body(j, carry):
        rows = pl.ds(pl.multiple_of(j * tk, tk), tk)
        bias = bias_sc[j]
        for h in range(B_HEADS):
            sl = slice(h * HEAD_DIM, (h + 1) * HEAD_DIM)
            s = _dot_nt(qb_ref[:, sl], kb_ref[rows, sl]) + bias
            _online_softmax_step(s, vb_ref[rows, sl], m_sc.at[h], l_sc.at[h], acc_sc.at[h])
        return carry

    lax.fori_loop(0, n_tiles, attn_body, 0)
    for h in range(B_HEADS):
        o_ref[:, h * HEAD_DIM:(h + 1) * HEAD_DIM] = _softmax_finish(l_sc.at[h], acc_sc.at[h])


def _dsa_attn(qb, qi, w, ki, kb, vb, n_sel):
    b, s, _ = qb.shape
    tq = min(s, 256)
    tk = min(s, 512)
    blk = lambda width: pl.BlockSpec((None, tq, width), lambda bi, i: (bi, i, 0))
    res = lambda width: pl.BlockSpec((None, s, width), lambda bi, i: (bi, 0, 0))
    return pl.pallas_call(
        functools.partial(_dsa_kernel, tq=tq, tk=tk, n_sel=n_sel),
        grid=(b, s // tq),
        in_specs=[blk(B_WIDTH), blk(IDX_HEADS * IDX_DIM), blk(LANES), res(LANES), res(B_WIDTH), res(B_WIDTH)],
        out_specs=blk(B_WIDTH),
        out_shape=jax.ShapeDtypeStruct((b, s, B_WIDTH), jnp.float32),
        scratch_shapes=[pltpu.VMEM((s // tk, tk, tq), jnp.int32),
                        pltpu.VMEM((s // tk, tq, tk), jnp.float32),
                        pltpu.VMEM((IDX_HEADS * tq, LANES), MXU_DTYPE),
                        pltpu.VMEM((B_HEADS, tq, LANES), jnp.float32),
                        pltpu.VMEM((B_HEADS, tq, LANES), jnp.float32),
                        pltpu.VMEM((B_HEADS, tq, HEAD_DIM), jnp.float32)],
        compiler_params=_params(2),
        name="dsa_attn",
    )(qb, qi, w, ki, kb, vb)


def _out_kernel(oa_ref, ob_ref, oc_ref, g_ref, h_ref, p_ref, wo_ref, wpg_ref, wple_ref, gn_ref,
                *out_refs, emit_h):
    g = g_ref[...]
    sg = g * _sigmoid(g)
    h1 = h_ref[...]
    col = 0
    for o_ref in (oa_ref, ob_ref, oc_ref):
        width = o_ref.shape[1]
        mixed = (o_ref[...] * sg[:, col:col + width]).astype(MXU_DTYPE)
        h1 = h1 + _dot(mixed, wo_ref[col:col + width, :])
        col += width
    gate = _sigmoid(_dot(h1.astype(MXU_DTYPE), wpg_ref[...]))
    h2 = h1 + _dot(p_ref[...].astype(MXU_DTYPE), wple_ref[...]) * gate
    if emit_h:
        out_refs[0][...] = h2
    out_refs[-1][...] = _rms(h2, gn_ref[...]).astype(out_refs[-1].dtype)


def _out_proj(oa, ob, oc, g, h, p, wo, wpg, wple, gn, emit_h):
    n, d = h.shape
    tm = min(n, 256)
    row = lambda a: pl.BlockSpec((tm, a.shape[1]), lambda i: (i, 0))
    const = lambda a: pl.BlockSpec(a.shape, lambda i: (0,) * a.ndim, pipeline_mode=pl.Buffered(1))
    out_specs = [pl.BlockSpec((tm, d), lambda i: (i, 0))]
    out_shape = [jax.ShapeDtypeStruct((n, d), MXU_DTYPE if emit_h else jnp.float32)]
    if emit_h:
        out_specs = [pl.BlockSpec((tm, d), lambda i: (i, 0))] + out_specs
        out_shape = [jax.ShapeDtypeStruct((n, d), jnp.float32)] + out_shape
    return pl.pallas_call(
        functools.partial(_out_kernel, emit_h=emit_h),
        grid=(n // tm,),
        in_specs=[row(oa), row(ob), row(oc), row(g), row(h), row(p), const(wo), const(wpg), const(wple),
                  pl.BlockSpec((1, d), lambda i: (0, 0))],
        out_specs=out_specs,
        out_shape=out_shape,
        compiler_params=_params(1),
        name="out_proj",
    )(oa, ob, oc, g, h, p, wo, wpg, wple, gn)


def _in_cols(name):
    k = IN_NAMES.index(name)
    start = sum(IN_SPLITS[:k])
    return start, start + IN_SPLITS[k]


def _gather_cols(w_in, parts):
    cols = []
    for part in parts:
        if isinstance(part, int):
            cols.append(jnp.zeros(w_in.shape[:2] + (part,), w_in.dtype))
        else:
            a, b = _in_cols(part)
            cols.append(w_in[:, :, a:b])
    return jnp.concatenate(cols, axis=-1).astype(MXU_DTYPE)


_SEG_G = ((0, D_MIX, None, None),)
_SEG_B = ((0, B_WIDTH, PAT_DSA, HEAD_DIM ** -0.5 * LOG2E), (B_WIDTH, B_WIDTH, PAT_DSA, None),
          (2 * B_WIDTH, B_WIDTH, None, None), (3 * B_WIDTH, IDX_HEADS * IDX_DIM, PAT_SMALL, None),
          (3 * B_WIDTH + IDX_HEADS * IDX_DIM, LANES, PAT_SMALL, None),
          (3 * B_WIDTH + IDX_HEADS * IDX_DIM + LANES, LANES, None, None))
_SEG_C = ((0, C_WIDTH, PAT_SMALL, C_QK ** -0.5 * LOG2E), (C_WIDTH, C_WIDTH, PAT_SMALL, None),
          (2 * C_WIDTH, C_WIDTH, None, None))


def kernel(x, p, positions, w_in, w_uq, w_ukv, w_o, norm_g, q_norm_g, kv_norm_g,
           lam_q1, lam_k1, lam_q2, lam_k2, subln_g, w_ple, w_pg, final_g):
    b, s, d = x.shape
    n = b * s
    f32 = jnp.float32
    n_sel = min(TOPK_MAX, s // 4)

    w_a = _gather_cols(w_in, ("c_q", "c_kv", "k_rope", LANES - A_ROPE))
    w_g = _gather_cols(w_in, ("g_a", "g_b", "g_c"))
    w_b = _gather_cols(w_in, ("q_b", "k_b", "v_b", "q_idx", "k_idx", "k_idx", "w_idx", LANES - IDX_HEADS))
    w_c = _gather_cols(w_in, ("q_c", "k_c", "v_c"))
    uq = w_uq.reshape(DEPTH, A_Q_RANK, A_HEADS, A_NOPE + A_ROPE)
    uq = jnp.pad(uq, ((0, 0), (0, 0), (0, 0), (0, A_QK_PAD - A_NOPE - A_ROPE)))
    uq = uq.reshape(DEPTH, A_Q_RANK, A_HEADS * A_QK_PAD).astype(MXU_DTYPE)
    ukv = w_ukv.reshape(DEPTH, A_KV_RANK, A_HEADS, A_NOPE + A_V)
    ukv = jnp.concatenate([ukv[..., :A_NOPE].reshape(DEPTH, A_KV_RANK, A_WIDTH),
                           ukv[..., A_NOPE:].reshape(DEPTH, A_KV_RANK, A_WIDTH)], axis=-1).astype(MXU_DTYPE)
    w_o_c = w_o.astype(MXU_DTYPE)
    w_pg_c = w_pg.astype(MXU_DTYPE)
    w_ple_c = w_ple.astype(MXU_DTYPE)
    pad_lanes = lambda v: jnp.pad(v, (0, LANES - v.shape[0]))

    tabs = _rope_tables(positions.astype(f32).reshape(n, 1))
    h = x.reshape(n, d)
    u = _norm(h, norm_g[0], MXU_DTYPE)
    out = None
    for i in range(DEPTH):
        q_a, k_a, v_a = _mla_proj(u, w_a[i], q_norm_g[i].reshape(1, -1), kv_norm_g[i].reshape(1, -1),
                                  uq[i], ukv[i], tabs)
        (g,) = _proj(u, w_g[i], None, _SEG_G, (f32,), "proj_gates")
        q_b, k_b, v_b, q_i, k_i, w_i = _proj(u, w_b[i], tabs, _SEG_B,
                                             (MXU_DTYPE,) * 5 + (f32,), "proj_dsa")
        q_c, k_c, v_c = _proj(u, w_c[i], tabs, _SEG_C, (MXU_DTYPE,) * 3, "proj_diff")

        r3 = lambda a: a.reshape(b, s, a.shape[-1])
        o_a = _mla_attn(r3(q_a), r3(k_a), r3(v_a))
        o_b = _dsa_attn(r3(q_b), r3(q_i), r3(w_i), r3(k_i), r3(k_b), r3(v_b), n_sel)
        lam_init = 0.8 - 0.6 * math.exp(-0.3 * i)
        consts = jnp.stack([pad_lanes(lam_q1[i]), pad_lanes(lam_k1[i]), pad_lanes(lam_q2[i]),
                            pad_lanes(lam_k2[i]), jnp.full((LANES,), lam_init, f32), subln_g[i],
                            jnp.zeros((LANES,), f32), jnp.zeros((LANES,), f32)]).astype(f32)
        o_c = _diff_attn(r3(q_c), r3(k_c), r3(v_c), consts)

        last = i == DEPTH - 1
        gn = (final_g if last else norm_g[i + 1]).reshape(1, d)
        res = _out_proj(o_a.reshape(n, -1), o_b.reshape(n, -1), o_c.reshape(n, -1), g, h,
                        p[i].reshape(n, PLE_DIM), w_o_c[i], w_pg_c[i], w_ple_c[i], gn, emit_h=not last)
        if last:
            out = res[0]
        else:
            h, u = res
    return out.reshape(b, s, d)
```

```python
import functools
import math

import jax
import jax.numpy as jnp
import numpy as np
from jax import lax
from jax.experimental import pallas as pl
from jax.experimental.pallas import tpu as pltpu

D_MODEL = 2048
DEPTH = 4
CHUNK = 64
ROPE_THETA = 500000.0
NORM_EPS = 1e-6
NEG_INF = -1e30
PLE_DIM = 256
HEAD_DIM = 128

A_HEADS = 6
A_Q_RANK = 384
A_KV_RANK = 256
A_NOPE = 128
A_ROPE = 64
A_V = 128
A_WIDTH = A_HEADS * A_V
B_HEADS = 5
B_WIDTH = B_HEADS * HEAD_DIM
IDX_HEADS = 8
IDX_DIM = 64
TOPK_MAX = 256
C_HEADS = 5
C_QK = 64
C_V = 128
C_WIDTH = C_HEADS * C_V
D_MIX = A_WIDTH + B_WIDTH + C_WIDTH

IN_NAMES = ("c_q", "c_kv", "k_rope", "g_a", "q_b", "k_b", "v_b", "q_idx", "k_idx", "w_idx", "g_b",
            "q_c", "k_c", "v_c", "g_c")
IN_SPLITS = (A_Q_RANK, A_KV_RANK, A_ROPE, A_WIDTH,
             B_WIDTH, B_WIDTH, B_WIDTH, IDX_HEADS * IDX_DIM, IDX_DIM, IDX_HEADS, B_WIDTH,
             C_HEADS * 2 * C_QK, C_HEADS * 2 * C_QK, C_WIDTH, C_WIDTH)

LANES = 128
A_QK_PAD = 256
INT_MIN = -2 ** 31
LOG2E = math.log2(math.e)
MXU_DTYPE = jnp.bfloat16
VMEM_LIMIT = 56 * 1024 * 1024

ROPE_PATTERNS = ((A_ROPE, 64), (HEAD_DIM // 4, 128), (IDX_DIM // 4, 64))
PAT_MLA, PAT_DSA, PAT_SMALL = 0, 1, 2

_NT = (((1,), (1,)), ((), ()))


def _params(n_axes):
    return pltpu.CompilerParams(dimension_semantics=("arbitrary",) * n_axes,
                                vmem_limit_bytes=VMEM_LIMIT)


def _dot(a, b):
    return jnp.dot(a, b, preferred_element_type=jnp.float32)


def _dot_nt(a, b):
    return lax.dot_general(a, b, _NT, preferred_element_type=jnp.float32)


def _sigmoid(x):
    return 1.0 / (1.0 + jnp.exp(-x))


def _rms(x, g):
    return x * lax.rsqrt(jnp.mean(x * x, axis=-1, keepdims=True) + NORM_EPS) * g


def _rope_pattern_rows():
    freq = []
    m_up = np.zeros((3, LANES), np.float32)
    m_dn = np.zeros((3, LANES), np.float32)
    for p, (n_rot, period) in enumerate(ROPE_PATTERNS):
        half = n_rot // 2
        inv = 1.0 / (ROPE_THETA ** (jnp.arange(half, dtype=jnp.float32) * (2.0 / n_rot)))
        one = jnp.concatenate([inv, inv, jnp.zeros((period - n_rot,), jnp.float32)])
        freq.append(jnp.tile(one, LANES // period))
        for lane in range(LANES):
            m = lane % period
            m_up[p, lane] = float(m < half)
            m_dn[p, lane] = float(half <= m < n_rot)
    return jnp.stack(freq), jnp.asarray(m_up), jnp.asarray(m_dn)


def _rope_tab_kernel(pos_ref, f_ref, mu_ref, md_ref, c_ref, sa_ref, sb_ref):
    pos = pos_ref[...]
    for p in range(3):
        ang = pos * f_ref[p:p + 1, :]
        sin = jnp.sin(ang)
        c_ref[p] = jnp.cos(ang)
        sa_ref[p] = -sin * mu_ref[p:p + 1, :]
        sb_ref[p] = sin * md_ref[p:p + 1, :]


def _rope_tables(pos_f32):
    n = pos_f32.shape[0]
    tm = min(n, 1024)
    freq, m_up, m_dn = _rope_pattern_rows()
    row = pl.BlockSpec((3, LANES), lambda i: (0, 0))
    tab = pl.BlockSpec((3, tm, LANES), lambda i: (0, i, 0))
    shape = jax.ShapeDtypeStruct((3, n, LANES), jnp.float32)
    return pl.pallas_call(
        _rope_tab_kernel,
        grid=(n // tm,),
        in_specs=[pl.BlockSpec((tm, 1), lambda i: (i, 0)), row, row, row],
        out_specs=[tab, tab, tab],
        out_shape=[shape, shape, shape],
        compiler_params=_params(1),
        name="rope_tables",
    )(pos_f32, freq, m_up, m_dn)


def _rope128(x, c, sa, sb, half):
    return x * c + pltpu.roll(x, LANES - half, 1) * sa + pltpu.roll(x, half, 1) * sb


def _norm_kernel(x_ref, g_ref, o_ref):
    o_ref[...] = _rms(x_ref[...], g_ref[...]).astype(o_ref.dtype)


def _norm(x, g, out_dtype):
    n, d = x.shape
    tm = min(n, 512)
    return pl.pallas_call(
        _norm_kernel,
        grid=(n // tm,),
        in_specs=[pl.BlockSpec((tm, d), lambda i: (i, 0)), pl.BlockSpec((1, d), lambda i: (0, 0))],
        out_specs=pl.BlockSpec((tm, d), lambda i: (i, 0)),
        out_shape=jax.ShapeDtypeStruct((n, d), out_dtype),
        compiler_params=_params(1),
        name="rms_norm",
    )(x, g.reshape(1, d))


def _proj_kernel(*refs, segs, with_tables):
    u_ref, w_ref = refs[0], refs[1]
    if with_tables:
        c_ref, sa_ref, sb_ref = refs[2:5]
        outs = refs[5:]
    else:
        outs = refs[2:]
    u = u_ref[...]
    for (c0, width, pat, scale), o_ref in zip(segs, outs):
        z = _dot(u, w_ref[:, c0:c0 + width])
        if pat is None and scale is None:
            o_ref[...] = z.astype(o_ref.dtype)
            continue
        for ch in range(width // LANES):
            sl = slice(ch * LANES, (ch + 1) * LANES)
            x = z[:, sl]
            if pat is not None:
                x = _rope128(x, c_ref[pat], sa_ref[pat], sb_ref[pat], ROPE_PATTERNS[pat][0] // 2)
            if scale is not None:
                x = x * scale
            o_ref[:, sl] = x.astype(o_ref.dtype)


def _proj(u, w, tabs, segs, out_dtypes, name):
    n, d = u.shape
    tm = min(n, 512)
    width = w.shape[1]
    in_specs = [pl.BlockSpec((tm, d), lambda i: (i, 0)), pl.BlockSpec((d, width), lambda i: (0, 0))]
    args = [u, w]
    if tabs is not None:
        in_specs += [pl.BlockSpec((3, tm, LANES), lambda i: (0, i, 0))] * 3
        args += list(tabs)
    out_specs = [pl.BlockSpec((tm, s[1]), lambda i: (i, 0)) for s in segs]
    out_shape = [jax.ShapeDtypeStruct((n, s[1]), dt) for s, dt in zip(segs, out_dtypes)]
    return pl.pallas_call(
        functools.partial(_proj_kernel, segs=tuple(segs), with_tables=tabs is not None),
        grid=(n // tm,),
        in_specs=in_specs,
        out_specs=out_specs,
        out_shape=out_shape,
        compiler_params=_params(1),
        name=name,
    )(*args)


def _mla_proj_kernel(u_ref, w_ref, qg_ref, kvg_ref, wuq_ref, wukv_ref, c_ref, sa_ref, sb_ref,
                     q_ref, k_ref, v_ref):
    z = _dot(u_ref[...], w_ref[...])
    c, sa, sb = c_ref[PAT_MLA], sa_ref[PAT_MLA], sb_ref[PAT_MLA]
    half = A_ROPE // 2
    scale = (A_NOPE + A_ROPE) ** -0.5 * LOG2E
    c_q = z[:, :A_Q_RANK]
    c_kv = z[:, A_Q_RANK:A_Q_RANK + A_KV_RANK]
    k_rope = z[:, A_Q_RANK + A_KV_RANK:]
    q = _dot(_rms(c_q, qg_ref[...]).astype(MXU_DTYPE), wuq_ref[...])
    kv = _dot(_rms(c_kv, kvg_ref[...]).astype(MXU_DTYPE), wukv_ref[...])
    k_pe = _rope128(k_rope, c, sa, sb, half).astype(k_ref.dtype)
    for h in range(A_HEADS):
        o = h * A_QK_PAD
        q_ref[:, o:o + A_NOPE] = (q[:, o:o + A_NOPE] * scale).astype(q_ref.dtype)
        q_rot = _rope128(q[:, o + A_NOPE:o + A_QK_PAD], c, sa, sb, half)
        q_ref[:, o + A_NOPE:o + A_QK_PAD] = (q_rot * scale).astype(q_ref.dtype)
        k_ref[:, o:o + A_NOPE] = kv[:, h * A_NOPE:(h + 1) * A_NOPE].astype(k_ref.dtype)
        k_ref[:, o + A_NOPE:o + A_QK_PAD] = k_pe
    v_ref[...] = kv[:, A_WIDTH:].astype(v_ref.dtype)


def _mla_proj(u, w, qg, kvg, wuq, wukv, tabs):
    n, d = u.shape
    tm = min(n, 512)
    full = lambda a: pl.BlockSpec(a.shape, lambda i: (0,) * a.ndim)
    tab = pl.BlockSpec((3, tm, LANES), lambda i: (0, i, 0))
    qk = A_HEADS * A_QK_PAD
    return pl.pallas_call(
        _mla_proj_kernel,
        grid=(n // tm,),
        in_specs=[pl.BlockSpec((tm, d), lambda i: (i, 0)), full(w), full(qg), full(kvg), full(wuq),
                  full(wukv), tab, tab, tab],
        out_specs=[pl.BlockSpec((tm, qk), lambda i: (i, 0)), pl.BlockSpec((tm, qk), lambda i: (i, 0)),
                   pl.BlockSpec((tm, A_WIDTH), lambda i: (i, 0))],
        out_shape=[jax.ShapeDtypeStruct((n, qk), MXU_DTYPE), jax.ShapeDtypeStruct((n, qk), MXU_DTYPE),
                   jax.ShapeDtypeStruct((n, A_WIDTH), MXU_DTYPE)],
        compiler_params=_params(1),
        name="mla_proj",
    )(u, w, qg, kvg, wuq, wukv, *tabs)


def _diag_mask(tq, tk, q0, k0):
    qc = (q0 + lax.broadcasted_iota(jnp.int32, (tq, tk), 0)) // CHUNK
    kc = (k0 + lax.broadcasted_iota(jnp.int32, (tq, tk), 1)) // CHUNK
    return kc <= qc


def _softmax_init(m_ref, l_ref, acc_ref):
    m_ref[...] = jnp.full(m_ref.shape, NEG_INF, jnp.float32)
    l_ref[...] = jnp.zeros(l_ref.shape, jnp.float32)
    acc_ref[...] = jnp.zeros(acc_ref.shape, jnp.float32)


def _online_softmax_step(s, v, m_ref, l_ref, acc_ref):
    chunks = [s[:, c * LANES:(c + 1) * LANES] for c in range(s.shape[1] // LANES)]
    cmax = functools.reduce(jnp.maximum, chunks)
    m_prev = m_ref[...]
    m_new = jnp.maximum(m_prev, jnp.max(cmax, axis=-1, keepdims=True))
    alpha = jnp.exp2(m_prev - m_new)
    ps = [jnp.exp2(c - m_new) for c in chunks]
    l_ref[...] = alpha * l_ref[...] + functools.reduce(jnp.add, ps)
    p = jnp.concatenate([x.astype(v.dtype) for x in ps], axis=1)
    acc_ref[...] = alpha * acc_ref[...] + _dot(p, v)
    m_ref[...] = m_new


def _softmax_finish(l_ref, acc_ref):
    return acc_ref[...] / jnp.sum(l_ref[...], axis=-1, keepdims=True)


MLA_HEADS_PER_STEP = 6


def _mla_attn_kernel(q_ref, k_ref, v_ref, o_ref, m_sc, l_sc, acc_sc, *, tq):
    i = pl.program_id(2)
    _softmax_init(m_sc, l_sc, acc_sc)

    def tile(j, masked):
        rows = pl.ds(pl.multiple_of(j * tq, tq), tq)
        for h in range(MLA_HEADS_PER_STEP):
            qk = slice(h * A_QK_PAD, (h + 1) * A_QK_PAD)
            s = _dot_nt(q_ref[:, qk], k_ref[rows, qk])
            if masked:
                s = jnp.where(_diag_mask(tq, tq, 0, 0), s, NEG_INF)
            _online_softmax_step(s, v_ref[rows, h * A_V:(h + 1) * A_V], m_sc.at[h], l_sc.at[h], acc_sc.at[h])

    def body(j, carry):
        tile(j, False)
        return carry

    lax.fori_loop(0, i, body, 0)
    tile(i, True)
    for h in range(MLA_HEADS_PER_STEP):
        o_ref[:, h * A_V:(h + 1) * A_V] = _softmax_finish(l_sc.at[h], acc_sc.at[h])


def _mla_attn(q, k, v):
    b, s, _ = q.shape
    tq = min(s, 512)
    hp = MLA_HEADS_PER_STEP
    stat = pltpu.VMEM((hp, tq, LANES), jnp.float32)
    return pl.pallas_call(
        functools.partial(_mla_attn_kernel, tq=tq),
        grid=(b, A_HEADS // hp, s // tq),
        in_specs=[pl.BlockSpec((None, tq, hp * A_QK_PAD), lambda bi, h, i: (bi, i, h)),
                  pl.BlockSpec((None, s, hp * A_QK_PAD), lambda bi, h, i: (bi, 0, h),
                               pipeline_mode=pl.Buffered(1)),
                  pl.BlockSpec((None, s, hp * A_V), lambda bi, h, i: (bi, 0, h),
                               pipeline_mode=pl.Buffered(1))],
        out_specs=pl.BlockSpec((None, tq, hp * A_V), lambda bi, h, i: (bi, i, h)),
        out_shape=jax.ShapeDtypeStruct((b, s, A_WIDTH), jnp.float32),
        scratch_shapes=[stat, stat, pltpu.VMEM((hp, tq, A_V), jnp.float32)],
        compiler_params=_params(3),
        name="mla_attn",
    )(q, k, v)


def _diff_attn_kernel(q_ref, k_ref, v_ref, c_ref, o_ref, qs_sc, m_sc, l_sc, acc_sc, *, tq):
    i = pl.program_id(1)
    lane = lax.broadcasted_iota(jnp.int32, (tq, 2 * C_QK), 1)
    for h in range(C_HEADS):
        q = q_ref[:, h * 2 * C_QK:(h + 1) * 2 * C_QK]
        qs_sc[2 * h] = jnp.where(lane < C_QK, q, jnp.zeros_like(q))
        qs_sc[2 * h + 1] = jnp.where(lane >= C_QK, q, jnp.zeros_like(q))
    _softmax_init(m_sc, l_sc, acc_sc)

    def tile(j, masked):
        rows = pl.ds(pl.multiple_of(j * tq, tq), tq)
        for h in range(C_HEADS):
            k = k_ref[rows, h * 2 * C_QK:(h + 1) * 2 * C_QK]
            v = v_ref[rows, h * C_V:(h + 1) * C_V]
            for t in range(2):
                n = 2 * h + t
                s = _dot_nt(qs_sc[n], k)
                if masked:
                    s = jnp.where(_diag_mask(tq, tq, 0, 0), s, NEG_INF)
                _online_softmax_step(s, v, m_sc.at[n], l_sc.at[n], acc_sc.at[n])

    def body(j, carry):
        tile(j, False)
        return carry

    lax.fori_loop(0, i, body, 0)
    tile(i, True)

    c = c_ref[...]
    lam_init = c[4:5, 0:1]
    lam = (jnp.exp(jnp.sum(c[0:1] * c[1:2], axis=-1, keepdims=True))
           - jnp.exp(jnp.sum(c[2:3] * c[3:4], axis=-1, keepdims=True)) + lam_init)
    for h in range(C_HEADS):
        o = (_softmax_finish(l_sc.at[2 * h], acc_sc.at[2 * h])
             - lam * _softmax_finish(l_sc.at[2 * h + 1], acc_sc.at[2 * h + 1]))
        o_ref[:, h * C_V:(h + 1) * C_V] = _rms(o, c[5:6]) * (1.0 - lam_init)


def _diff_attn(q, k, v, consts):
    b, s, _ = q.shape
    tq = min(s, 512)
    n_chain = 2 * C_HEADS
    stat = pltpu.VMEM((n_chain, tq, LANES), jnp.float32)
    resident = lambda width: pl.BlockSpec((None, s, width), lambda bi, i: (bi, 0, 0),
                                          pipeline_mode=pl.Buffered(1))
    return pl.pallas_call(
        functools.partial(_diff_attn_kernel, tq=tq),
        grid=(b, s // tq),
        in_specs=[pl.BlockSpec((None, tq, C_WIDTH), lambda bi, i: (bi, i, 0)),
                  resident(C_WIDTH), resident(C_WIDTH),
                  pl.BlockSpec((8, LANES), lambda bi, i: (0, 0))],
        out_specs=pl.BlockSpec((None, tq, C_WIDTH), lambda bi, i: (bi, i, 0)),
        out_shape=jax.ShapeDtypeStruct((b, s, C_WIDTH), jnp.float32),
        scratch_shapes=[pltpu.VMEM((n_chain, tq, 2 * C_QK), MXU_DTYPE), stat, stat,
                        pltpu.VMEM((n_chain, tq, C_V), jnp.float32)],
        compiler_params=_params(2),
        name="diff_attn",
    )(q, k, v, consts)


SUBLANES = 8
COUNT_CHAINS = 4
PACKED_ROWS = 16
HALF_MIN = -2 ** 15


def _tree_sum(parts):
    while len(parts) > 1:
        parts = [a + b for a, b in zip(parts[0::2], parts[1::2])] + parts[len(parts) & ~1:]
    return parts[0]


def _dsa_kernel(qb_ref, qi_ref, w_ref, ki_ref, kb_ref, vb_ref, o_ref,
                keys_sc, hi_sc, lo_sc, bias_sc, qz_sc, m_sc, l_sc, acc_sc, *, tq, tk, n_sel):
    i = pl.program_id(1)
    n_tiles = (i * tq) // tk + 1
    f32 = jnp.float32

    lane = lax.broadcasted_iota(jnp.int32, (tq, LANES), 1)
    for h in range(IDX_HEADS):
        pair = qi_ref[:, (h // 2) * LANES:(h // 2 + 1) * LANES]
        keep = (lane < IDX_DIM) if h % 2 == 0 else (lane >= IDX_DIM)
        qz_sc[h * tq:(h + 1) * tq, :] = jnp.where(keep, pair, jnp.zeros_like(pair))
    w_t = w_ref[...].T

    def score_tile(j, masked):
        rows = pl.ds(pl.multiple_of(j * tk, tk), tk)
        rel = _dot_nt(ki_ref[rows, :], qz_sc[...])
        sc = jnp.zeros((tk, tq), f32)
        for h in range(IDX_HEADS):
            sc = sc + jnp.maximum(rel[:, h * tq:(h + 1) * tq], 0.0) * w_t[h:h + 1, :]
        if masked:
            kc = (j * tk + lax.broadcasted_iota(jnp.int32, (tk, tq), 0)) // CHUNK
            qc = (i * tq + lax.broadcasted_iota(jnp.int32, (tk, tq), 1)) // CHUNK
            sc = jnp.where(kc <= qc, sc, NEG_INF)
        bits = lax.bitcast_convert_type(sc, jnp.int32)
        key = jnp.where(bits < 0, INT_MIN - bits, bits)
        key = jnp.where(sc > 0.5 * NEG_INF, key, INT_MIN)
        keys_sc[j] = key
        hi_sc[j] = (key >> 16).astype(jnp.int16)

    def score_body(j, carry):
        score_tile(j, False)
        return carry

    lax.fori_loop(0, n_tiles - 1, score_body, 0)
    score_tile(n_tiles - 1, True)

    def count_ge(t):
        def body(j, accs):
            accs = list(accs)
            for n, r in enumerate(range(0, tk, SUBLANES)):
                hit = jnp.where(keys_sc[j, r:r + SUBLANES, :] >= t, 1.0, 0.0)
                accs[n % COUNT_CHAINS] = accs[n % COUNT_CHAINS] + hit
            return tuple(accs)
        zero = jnp.zeros((SUBLANES, tq), f32)
        accs = lax.fori_loop(0, n_tiles, body, (zero,) * COUNT_CHAINS)
        return jnp.sum(_tree_sum(list(accs)), axis=0, keepdims=True)

    def count16(ref, t, strict):
        t16 = t.astype(jnp.int16)
        one, zero = jnp.ones((), jnp.int16), jnp.zeros((), jnp.int16)

        def body(j, accs):
            accs = list(accs)
            for n, r in enumerate(range(0, tk, PACKED_ROWS)):
                x = ref[j, r:r + PACKED_ROWS, :]
                hit = jnp.where((x > t16) if strict else (x >= t16), one, zero)
                accs[n % COUNT_CHAINS] = accs[n % COUNT_CHAINS] + hit
            return tuple(accs)

        accs = lax.fori_loop(0, n_tiles, body, (jnp.zeros((PACKED_ROWS, tq), jnp.int16),) * COUNT_CHAINS)
        return jnp.sum(_tree_sum(list(accs)).astype(jnp.int32), axis=0, keepdims=True)

    def kth_largest16(ref, k_need):
        def step(it, t):
            cand = t + lax.shift_left(jnp.int32(1), jnp.int32(15) - it)
            return jnp.where(count16(ref, cand, False) >= k_need, cand, t)
        return lax.fori_loop(0, 16, step, jnp.full((1, tq), HALF_MIN, jnp.int32))

    thr_hi = kth_largest16(hi_sc, n_sel)
    k_lo = n_sel - count16(hi_sc, thr_hi, True)
    thr_hi16 = thr_hi.astype(jnp.int16)

    def lo_body(j, carry):
        lo = ((keys_sc[j] & 0xFFFF) + HALF_MIN).astype(jnp.int16)
        lo_sc[j] = jnp.where(hi_sc[j] == thr_hi16, lo, jnp.int16(HALF_MIN))
        return carry

    lax.fori_loop(0, n_tiles, lo_body, 0)
    thr_lo = kth_largest16(lo_sc, k_lo)
    thr = thr_hi * 65536 + (thr_lo - HALF_MIN)
    thr = jnp.maximum(thr, INT_MIN + 1)
    k_f = float(n_sel)
    cnt_ge = count_ge(thr)

    @pl.when(jnp.max(cnt_ge) > k_f)
    def _():
        keep_n = k_f - count_ge(thr + 1)
        tri = (lax.broadcasted_iota(jnp.int32, (tk, tk), 0)
               >= lax.broadcasted_iota(jnp.int32, (tk, tk), 1)).astype(MXU_DTYPE)

        def body(j, carry):
            kt = keys_sc[j]
            eq = kt == thr
            rank = carry + _dot(tri, jnp.where(eq, 1.0, 0.0).astype(MXU_DTYPE))
            keys_sc[j] = jnp.where(eq & (rank > keep_n), INT_MIN, kt)
            return rank[tk - 1:tk, :]

        lax.fori_loop(0, n_tiles, body, jnp.zeros((1, tq), f32))

    def bias_body(j, carry):
        bias_sc[j] = jnp.where(keys_sc[j] >= thr, 0.0, NEG_INF).T
        return carry

    lax.fori_loop(0, n_tiles, bias_body, 0)
    _softmax_init(m_sc, l_sc, acc_sc)

    def attn_body(j, carry):
        rows = pl.ds(pl.multiple_of(j * tk, tk), tk)
        bias = bias_sc[j]
        for h in range(B_HEADS):
            sl = slice(h * HEAD_DIM, (h + 1) * HEAD_DIM)
            s = _dot_nt(qb_ref[:, sl], kb_ref[rows, sl]) + bias
            _online_softmax_step(s, vb_ref[rows, sl], m_sc.at[h], l_sc.at[h], acc_sc.at[h])
        return carry

    lax.fori_loop(0, n_tiles, attn_body, 0)
    for h in range(B_HEADS):
        o_ref[:, h * HEAD_DIM:(h + 1) * HEAD_DIM] = _softmax_finish(l_sc.at[h], acc_sc.at[h])


def _dsa_attn(qb, qi, w, ki, kb, vb, n_sel):
    b, s, _ = qb.shape
    tq = min(s, 256)
    tk = min(s, 512)
    blk = lambda width: pl.BlockSpec((None, tq, width), lambda bi, i: (bi, i, 0))
    res = lambda width: pl.BlockSpec((None, s, width), lambda bi, i: (bi, 0, 0), pipeline_mode=pl.Buffered(1))
    return pl.pallas_call(
        functools.partial(_dsa_kernel, tq=tq, tk=tk, n_sel=n_sel),
        grid=(b, s // tq),
        in_specs=[blk(B_WIDTH), blk(IDX_HEADS * IDX_DIM), blk(LANES), res(LANES), res(B_WIDTH), res(B_WIDTH)],
        out_specs=blk(B_WIDTH),
        out_shape=jax.ShapeDtypeStruct((b, s, B_WIDTH), jnp.float32),
        scratch_shapes=[pltpu.VMEM((s // tk, tk, tq), jnp.int32),
                        pltpu.VMEM((s // tk, tk, tq), jnp.int16),
                        pltpu.VMEM((s // tk, tk, tq), jnp.int16),
                        pltpu.VMEM((s // tk, tq, tk), jnp.float32),
                        pltpu.VMEM((IDX_HEADS * tq, LANES), MXU_DTYPE),
                        pltpu.VMEM((B_HEADS, tq, LANES), jnp.float32),
                        pltpu.VMEM((B_HEADS, tq, LANES), jnp.float32),
                        pltpu.VMEM((B_HEADS, tq, HEAD_DIM), jnp.float32)],
        compiler_params=_params(2),
        name="dsa_attn",
    )(qb, qi, w, ki, kb, vb)


def _out_kernel(oa_ref, ob_ref, oc_ref, g_ref, h_ref, p_ref, wo_ref, wpg_ref, wple_ref, gn_ref,
                *out_refs, emit_h):
    g = g_ref[...]
    sg = g * _sigmoid(g)
    h1 = h_ref[...]
    col = 0
    for o_ref in (oa_ref, ob_ref, oc_ref):
        width = o_ref.shape[1]
        mixed = (o_ref[...] * sg[:, col:col + width]).astype(MXU_DTYPE)
        h1 = h1 + _dot(mixed, wo_ref[col:col + width, :])
        col += width
    gate = _sigmoid(_dot(h1.astype(MXU_DTYPE), wpg_ref[...]))
    h2 = h1 + _dot(p_ref[...].astype(MXU_DTYPE), wple_ref[...]) * gate
    if emit_h:
        out_refs[0][...] = h2
    out_refs[-1][...] = _rms(h2, gn_ref[...]).astype(out_refs[-1].dtype)


def _out_proj(oa, ob, oc, g, h, p, wo, wpg, wple, gn, emit_h):
    n, d = h.shape
    tm = min(n, 256)
    row = lambda a: pl.BlockSpec((tm, a.shape[1]), lambda i: (i, 0))
    const = lambda a: pl.BlockSpec(a.shape, lambda i: (0,) * a.ndim, pipeline_mode=pl.Buffered(1))
    out_specs = [pl.BlockSpec((tm, d), lambda i: (i, 0))]
    out_shape = [jax.ShapeDtypeStruct((n, d), MXU_DTYPE if emit_h else jnp.float32)]
    if emit_h:
        out_specs = [pl.BlockSpec((tm, d), lambda i: (i, 0))] + out_specs
        out_shape = [jax.ShapeDtypeStruct((n, d), jnp.float32)] + out_shape
    return pl.pallas_call(
        functools.partial(_out_kernel, emit_h=emit_h),
        grid=(n // tm,),
        in_specs=[row(oa), row(ob), row(oc), row(g), row(h), row(p), const(wo), const(wpg), const(wple),
                  pl.BlockSpec((1, d), lambda i: (0, 0))],
        out_specs=out_specs,
        out_shape=out_shape,
        compiler_params=_params(1),
        name="out_proj",
    )(oa, ob, oc, g, h, p, wo, wpg, wple, gn)


def _in_cols(name):
    k = IN_NAMES.index(name)
    start = sum(IN_SPLITS[:k])
    return start, start + IN_SPLITS[k]


def _gather_cols(w_in, parts):
    cols = []
    for part in parts:
        if isinstance(part, int):
            cols.append(jnp.zeros(w_in.shape[:2] + (part,), w_in.dtype))
        else:
            a, b = _in_cols(part)
            cols.append(w_in[:, :, a:b])
    return jnp.concatenate(cols, axis=-1).astype(MXU_DTYPE)


_SEG_G = ((0, D_MIX, None, None),)
_SEG_B = ((0, B_WIDTH, PAT_DSA, HEAD_DIM ** -0.5 * LOG2E), (B_WIDTH, B_WIDTH, PAT_DSA, None),
          (2 * B_WIDTH, B_WIDTH, None, None), (3 * B_WIDTH, IDX_HEADS * IDX_DIM, PAT_SMALL, None),
          (3 * B_WIDTH + IDX_HEADS * IDX_DIM, LANES, PAT_SMALL, None),
          (3 * B_WIDTH + IDX_HEADS * IDX_DIM + LANES, LANES, None, None))
_SEG_C = ((0, C_WIDTH, PAT_SMALL, C_QK ** -0.5 * LOG2E), (C_WIDTH, C_WIDTH, PAT_SMALL, None),
          (2 * C_WIDTH, C_WIDTH, None, None))


def kernel(x, p, positions, w_in, w_uq, w_ukv, w_o, norm_g, q_norm_g, kv_norm_g,
           lam_q1, lam_k1, lam_q2, lam_k2, subln_g, w_ple, w_pg, final_g):
    b, s, d = x.shape
    n = b * s
    f32 = jnp.float32
    n_sel = min(TOPK_MAX, s // 4)

    w_a = _gather_cols(w_in, ("c_q", "c_kv", "k_rope", LANES - A_ROPE))
    w_g = _gather_cols(w_in, ("g_a", "g_b", "g_c"))
    w_b = _gather_cols(w_in, ("q_b", "k_b", "v_b", "q_idx", "k_idx", "k_idx", "w_idx", LANES - IDX_HEADS))
    w_c = _gather_cols(w_in, ("q_c", "k_c", "v_c"))
    uq = w_uq.reshape(DEPTH, A_Q_RANK, A_HEADS, A_NOPE + A_ROPE)
    uq = jnp.pad(uq, ((0, 0), (0, 0), (0, 0), (0, A_QK_PAD - A_NOPE - A_ROPE)))
    uq = uq.reshape(DEPTH, A_Q_RANK, A_HEADS * A_QK_PAD).astype(MXU_DTYPE)
    ukv = w_ukv.reshape(DEPTH, A_KV_RANK, A_HEADS, A_NOPE + A_V)
    ukv = jnp.concatenate([ukv[..., :A_NOPE].reshape(DEPTH, A_KV_RANK, A_WIDTH),
                           ukv[..., A_NOPE:].reshape(DEPTH, A_KV_RANK, A_WIDTH)], axis=-1).astype(MXU_DTYPE)
    w_o_c = w_o.astype(MXU_DTYPE)
    w_pg_c = w_pg.astype(MXU_DTYPE)
    w_ple_c = w_ple.astype(MXU_DTYPE)
    pad_lanes = lambda v: jnp.pad(v, (0, LANES - v.shape[0]))

    tabs = _rope_tables(positions.astype(f32).reshape(n, 1))
    h = x.reshape(n, d)
    u = _norm(h, norm_g[0], MXU_DTYPE)
    out = None
    for i in range(DEPTH):
        q_a, k_a, v_a = _mla_proj(u, w_a[i], q_norm_g[i].reshape(1, -1), kv_norm_g[i].reshape(1, -1),
                                  uq[i], ukv[i], tabs)
        (g,) = _proj(u, w_g[i], None, _SEG_G, (f32,), "proj_gates")
        q_b, k_b, v_b, q_i, k_i, w_i = _proj(u, w_b[i], tabs, _SEG_B,
                                             (MXU_DTYPE,) * 5 + (f32,), "proj_dsa")
        q_c, k_c, v_c = _proj(u, w_c[i], tabs, _SEG_C, (MXU_DTYPE,) * 3, "proj_diff")

        r3 = lambda a: a.reshape(b, s, a.shape[-1])
        o_a = _mla_attn(r3(q_a), r3(k_a), r3(v_a))
        o_b = _dsa_attn(r3(q_b), r3(q_i), r3(w_i), r3(k_i), r3(k_b), r3(v_b), n_sel)
        lam_init = 0.8 - 0.6 * math.exp(-0.3 * i)
        consts = jnp.stack([pad_lanes(lam_q1[i]), pad_lanes(lam_k1[i]), pad_lanes(lam_q2[i]),
                            pad_lanes(lam_k2[i]), jnp.full((LANES,), lam_init, f32), subln_g[i],
                            jnp.zeros((LANES,), f32), jnp.zeros((LANES,), f32)]).astype(f32)
        o_c = _diff_attn(r3(q_c), r3(k_c), r3(v_c), consts)

        last = i == DEPTH - 1
        gn = (final_g if last else norm_g[i + 1]).reshape(1, d)
        res = _out_proj(o_a.reshape(n, -1), o_b.reshape(n, -1), o_c.reshape(n, -1), g, h,
                        p[i].reshape(n, PLE_DIM), w_o_c[i], w_pg_c[i], w_ple_c[i], gn, emit_h=not last)
        if last:
            out = res[0]
        else:
            h, u = res
    return out.reshape(b, s, d)
```

```python
import functools
import math

import jax
import jax.numpy as jnp
import numpy as np
from jax import lax
from jax.experimental import pallas as pl
from jax.experimental.pallas import tpu as pltpu

D_MODEL = 2048
DEPTH = 4
CHUNK = 64
ROPE_THETA = 500000.0
NORM_EPS = 1e-6
NEG_INF = -1e30
PLE_DIM = 256
HEAD_DIM = 128

A_HEADS = 6
A_Q_RANK = 384
A_KV_RANK = 256
A_NOPE = 128
A_ROPE = 64
A_V = 128
A_WIDTH = A_HEADS * A_V
B_HEADS = 5
B_WIDTH = B_HEADS * HEAD_DIM
IDX_HEADS = 8
IDX_DIM = 64
TOPK_MAX = 256
C_HEADS = 5
C_QK = 64
C_V = 128
C_WIDTH = C_HEADS * C_V
D_MIX = A_WIDTH + B_WIDTH + C_WIDTH

IN_NAMES = ("c_q", "c_kv", "k_rope", "g_a", "q_b", "k_b", "v_b", "q_idx", "k_idx", "w_idx", "g_b",
            "q_c", "k_c", "v_c", "g_c")
IN_SPLITS = (A_Q_RANK, A_KV_RANK, A_ROPE, A_WIDTH,
             B_WIDTH, B_WIDTH, B_WIDTH, IDX_HEADS * IDX_DIM, IDX_DIM, IDX_HEADS, B_WIDTH,
             C_HEADS * 2 * C_QK, C_HEADS * 2 * C_QK, C_WIDTH, C_WIDTH)

LANES = 128
A_QK_PAD = 256
INT_MIN = -2 ** 31
LOG2E = math.log2(math.e)
MXU_DTYPE = jnp.bfloat16
VMEM_LIMIT = 56 * 1024 * 1024

ROPE_PATTERNS = ((A_ROPE, 64), (HEAD_DIM // 4, 128), (IDX_DIM // 4, 64))
PAT_MLA, PAT_DSA, PAT_SMALL = 0, 1, 2

_NT = (((1,), (1,)), ((), ()))


def _params(n_axes):
    return pltpu.CompilerParams(dimension_semantics=("arbitrary",) * n_axes,
                                vmem_limit_bytes=VMEM_LIMIT)


def _dot(a, b):
    return jnp.dot(a, b, preferred_element_type=jnp.float32)


def _dot_nt(a, b):
    return lax.dot_general(a, b, _NT, preferred_element_type=jnp.float32)


def _sigmoid(x):
    return 1.0 / (1.0 + jnp.exp(-x))


def _rms(x, g):
    return x * lax.rsqrt(jnp.mean(x * x, axis=-1, keepdims=True) + NORM_EPS) * g


def _rope_pattern_rows():
    freq = []
    m_up = np.zeros((3, LANES), np.float32)
    m_dn = np.zeros((3, LANES), np.float32)
    for p, (n_rot, period) in enumerate(ROPE_PATTERNS):
        half = n_rot // 2
        inv = 1.0 / (ROPE_THETA ** (jnp.arange(half, dtype=jnp.float32) * (2.0 / n_rot)))
        one = jnp.concatenate([inv, inv, jnp.zeros((period - n_rot,), jnp.float32)])
        freq.append(jnp.tile(one, LANES // period))
        for lane in range(LANES):
            m = lane % period
            m_up[p, lane] = float(m < half)
            m_dn[p, lane] = float(half <= m < n_rot)
    return jnp.stack(freq), jnp.asarray(m_up), jnp.asarray(m_dn)


def _rope_tab_kernel(pos_ref, f_ref, mu_ref, md_ref, c_ref, sa_ref, sb_ref):
    pos = pos_ref[...]
    for p in range(3):
        ang = pos * f_ref[p:p + 1, :]
        sin = jnp.sin(ang)
        c_ref[p] = jnp.cos(ang)
        sa_ref[p] = -sin * mu_ref[p:p + 1, :]
        sb_ref[p] = sin * md_ref[p:p + 1, :]


def _rope_tables(pos_f32):
    n = pos_f32.shape[0]
    tm = min(n, 1024)
    freq, m_up, m_dn = _rope_pattern_rows()
    row = pl.BlockSpec((3, LANES), lambda i: (0, 0))
    tab = pl.BlockSpec((3, tm, LANES), lambda i: (0, i, 0))
    shape = jax.ShapeDtypeStruct((3, n, LANES), jnp.float32)
    return pl.pallas_call(
        _rope_tab_kernel,
        grid=(n // tm,),
        in_specs=[pl.BlockSpec((tm, 1), lambda i: (i, 0)), row, row, row],
        out_specs=[tab, tab, tab],
        out_shape=[shape, shape, shape],
        compiler_params=_params(1),
        name="rope_tables",
    )(pos_f32, freq, m_up, m_dn)


def _rope128(x, c, sa, sb, half):
    return x * c + pltpu.roll(x, LANES - half, 1) * sa + pltpu.roll(x, half, 1) * sb


def _norm_kernel(x_ref, g_ref, o_ref):
    o_ref[...] = _rms(x_ref[...], g_ref[...]).astype(o_ref.dtype)


def _norm(x, g, out_dtype):
    n, d = x.shape
    tm = min(n, 512)
    return pl.pallas_call(
        _norm_kernel,
        grid=(n // tm,),
        in_specs=[pl.BlockSpec((tm, d), lambda i: (i, 0)), pl.BlockSpec((1, d), lambda i: (0, 0))],
        out_specs=pl.BlockSpec((tm, d), lambda i: (i, 0)),
        out_shape=jax.ShapeDtypeStruct((n, d), out_dtype),
        compiler_params=_params(1),
        name="rms_norm",
    )(x, g.reshape(1, d))


def _proj_kernel(*refs, segs, with_tables):
    u_ref, w_ref = refs[0], refs[1]
    if with_tables:
        c_ref, sa_ref, sb_ref = refs[2:5]
        outs = refs[5:]
    else:
        outs = refs[2:]
    u = u_ref[...]
    for (c0, width, pat, scale), o_ref in zip(segs, outs):
        z = _dot(u, w_ref[:, c0:c0 + width])
        if pat is None and scale is None:
            o_ref[...] = z.astype(o_ref.dtype)
            continue
        for ch in range(width // LANES):
            sl = slice(ch * LANES, (ch + 1) * LANES)
            x = z[:, sl]
            if pat is not None:
                x = _rope128(x, c_ref[pat], sa_ref[pat], sb_ref[pat], ROPE_PATTERNS[pat][0] // 2)
            if scale is not None:
                x = x * scale
            o_ref[:, sl] = x.astype(o_ref.dtype)


def _proj(u, w, tabs, segs, out_dtypes, name):
    n, d = u.shape
    tm = min(n, 512)
    width = w.shape[1]
    in_specs = [pl.BlockSpec((tm, d), lambda i: (i, 0)), pl.BlockSpec((d, width), lambda i: (0, 0))]
    args = [u, w]
    if tabs is not None:
        in_specs += [pl.BlockSpec((3, tm, LANES), lambda i: (0, i, 0))] * 3
        args += list(tabs)
    out_specs = [pl.BlockSpec((tm, s[1]), lambda i: (i, 0)) for s in segs]
    out_shape = [jax.ShapeDtypeStruct((n, s[1]), dt) for s, dt in zip(segs, out_dtypes)]
    return pl.pallas_call(
        functools.partial(_proj_kernel, segs=tuple(segs), with_tables=tabs is not None),
        grid=(n // tm,),
        in_specs=in_specs,
        out_specs=out_specs,
        out_shape=out_shape,
        compiler_params=_params(1),
        name=name,
    )(*args)


def _mla_proj_kernel(u_ref, w_ref, qg_ref, kvg_ref, wuq_ref, wukv_ref, c_ref, sa_ref, sb_ref,
                     q_ref, k_ref, v_ref):
    z = _dot(u_ref[...], w_ref[...])
    c, sa, sb = c_ref[PAT_MLA], sa_ref[PAT_MLA], sb_ref[PAT_MLA]
    half = A_ROPE // 2
    scale = (A_NOPE + A_ROPE) ** -0.5 * LOG2E
    c_q = z[:, :A_Q_RANK]
    c_kv = z[:, A_Q_RANK:A_Q_RANK + A_KV_RANK]
    k_rope = z[:, A_Q_RANK + A_KV_RANK:]
    q = _dot(_rms(c_q, qg_ref[...]).astype(MXU_DTYPE), wuq_ref[...])
    kv = _dot(_rms(c_kv, kvg_ref[...]).astype(MXU_DTYPE), wukv_ref[...])
    k_pe = _rope128(k_rope, c, sa, sb, half).astype(k_ref.dtype)
    for h in range(A_HEADS):
        o = h * A_QK_PAD
        q_ref[:, o:o + A_NOPE] = (q[:, o:o + A_NOPE] * scale).astype(q_ref.dtype)
        q_rot = _rope128(q[:, o + A_NOPE:o + A_QK_PAD], c, sa, sb, half)
        q_ref[:, o + A_NOPE:o + A_QK_PAD] = (q_rot * scale).astype(q_ref.dtype)
        k_ref[:, o:o + A_NOPE] = kv[:, h * A_NOPE:(h + 1) * A_NOPE].astype(k_ref.dtype)
        k_ref[:, o + A_NOPE:o + A_QK_PAD] = k_pe
    v_ref[...] = kv[:, A_WIDTH:].astype(v_ref.dtype)


def _mla_proj(u, w, qg, kvg, wuq, wukv, tabs):
    n, d = u.shape
    tm = min(n, 512)
    full = lambda a: pl.BlockSpec(a.shape, lambda i: (0,) * a.ndim)
    tab = pl.BlockSpec((3, tm, LANES), lambda i: (0, i, 0))
    qk = A_HEADS * A_QK_PAD
    return pl.pallas_call(
        _mla_proj_kernel,
        grid=(n // tm,),
        in_specs=[pl.BlockSpec((tm, d), lambda i: (i, 0)), full(w), full(qg), full(kvg), full(wuq),
                  full(wukv), tab, tab, tab],
        out_specs=[pl.BlockSpec((tm, qk), lambda i: (i, 0)), pl.BlockSpec((tm, qk), lambda i: (i, 0)),
                   pl.BlockSpec((tm, A_WIDTH), lambda i: (i, 0))],
        out_shape=[jax.ShapeDtypeStruct((n, qk), MXU_DTYPE), jax.ShapeDtypeStruct((n, qk), MXU_DTYPE),
                   jax.ShapeDtypeStruct((n, A_WIDTH), MXU_DTYPE)],
        compiler_params=_params(1),
        name="mla_proj",
    )(u, w, qg, kvg, wuq, wukv, *tabs)


def _diag_mask(tq, tk, q0, k0):
    qc = (q0 + lax.broadcasted_iota(jnp.int32, (tq, tk), 0)) // CHUNK
    kc = (k0 + lax.broadcasted_iota(jnp.int32, (tq, tk), 1)) // CHUNK
    return kc <= qc


def _softmax_init(m_ref, l_ref, acc_ref):
    m_ref[...] = jnp.full(m_ref.shape, NEG_INF, jnp.float32)
    l_ref[...] = jnp.zeros(l_ref.shape, jnp.float32)
    acc_ref[...] = jnp.zeros(acc_ref.shape, jnp.float32)


def _online_softmax_step(s, v, m_ref, l_ref, acc_ref):
    chunks = [s[:, c * LANES:(c + 1) * LANES] for c in range(s.shape[1] // LANES)]
    cmax = functools.reduce(jnp.maximum, chunks)
    m_prev = m_ref[...]
    m_new = jnp.maximum(m_prev, jnp.max(cmax, axis=-1, keepdims=True))
    alpha = jnp.exp2(m_prev - m_new)
    ps = [jnp.exp2(c - m_new) for c in chunks]
    l_ref[...] = alpha * l_ref[...] + functools.reduce(jnp.add, ps)
    p = jnp.concatenate([x.astype(v.dtype) for x in ps], axis=1)
    acc_ref[...] = alpha * acc_ref[...] + _dot(p, v)
    m_ref[...] = m_new


def _softmax_finish(l_ref, acc_ref):
    return acc_ref[...] / jnp.sum(l_ref[...], axis=-1, keepdims=True)


MLA_HEADS_PER_STEP = 6


def _mla_attn_kernel(q_ref, k_ref, v_ref, o_ref, m_sc, l_sc, acc_sc, *, tq):
    i = pl.program_id(2)
    _softmax_init(m_sc, l_sc, acc_sc)

    def tile(j, masked):
        rows = pl.ds(pl.multiple_of(j * tq, tq), tq)
        for h in range(MLA_HEADS_PER_STEP):
            qk = slice(h * A_QK_PAD, (h + 1) * A_QK_PAD)
            s = _dot_nt(q_ref[:, qk], k_ref[rows, qk])
            if masked:
                s = jnp.where(_diag_mask(tq, tq, 0, 0), s, NEG_INF)
            _online_softmax_step(s, v_ref[rows, h * A_V:(h + 1) * A_V], m_sc.at[h], l_sc.at[h], acc_sc.at[h])

    def body(j, carry):
        tile(j, False)
        return carry

    lax.fori_loop(0, i, body, 0)
    tile(i, True)
    for h in range(MLA_HEADS_PER_STEP):
        o_ref[:, h * A_V:(h + 1) * A_V] = _softmax_finish(l_sc.at[h], acc_sc.at[h])


def _mla_attn(q, k, v):
    b, s, _ = q.shape
    tq = min(s, 512)
    hp = MLA_HEADS_PER_STEP
    stat = pltpu.VMEM((hp, tq, LANES), jnp.float32)
    return pl.pallas_call(
        functools.partial(_mla_attn_kernel, tq=tq),
        grid=(b, A_HEADS // hp, s // tq),
        in_specs=[pl.BlockSpec((None, tq, hp * A_QK_PAD), lambda bi, h, i: (bi, i, h)),
                  pl.BlockSpec((None, s, hp * A_QK_PAD), lambda bi, h, i: (bi, 0, h),
                               pipeline_mode=pl.Buffered(1)),
                  pl.BlockSpec((None, s, hp * A_V), lambda bi, h, i: (bi, 0, h),
                               pipeline_mode=pl.Buffered(1))],
        out_specs=pl.BlockSpec((None, tq, hp * A_V), lambda bi, h, i: (bi, i, h)),
        out_shape=jax.ShapeDtypeStruct((b, s, A_WIDTH), jnp.float32),
        scratch_shapes=[stat, stat, pltpu.VMEM((hp, tq, A_V), jnp.float32)],
        compiler_params=_params(3),
        name="mla_attn",
    )(q, k, v)


def _diff_attn_kernel(q_ref, k_ref, v_ref, c_ref, o_ref, qs_sc, m_sc, l_sc, acc_sc, *, tq):
    i = pl.program_id(1)
    lane = lax.broadcasted_iota(jnp.int32, (tq, 2 * C_QK), 1)
    for h in range(C_HEADS):
        q = q_ref[:, h * 2 * C_QK:(h + 1) * 2 * C_QK]
        qs_sc[2 * h] = jnp.where(lane < C_QK, q, jnp.zeros_like(q))
        qs_sc[2 * h + 1] = jnp.where(lane >= C_QK, q, jnp.zeros_like(q))
    _softmax_init(m_sc, l_sc, acc_sc)

    def tile(j, masked):
        rows = pl.ds(pl.multiple_of(j * tq, tq), tq)
        for h in range(C_HEADS):
            k = k_ref[rows, h * 2 * C_QK:(h + 1) * 2 * C_QK]
            v = v_ref[rows, h * C_V:(h + 1) * C_V]
            for t in range(2):
                n = 2 * h + t
                s = _dot_nt(qs_sc[n], k)
                if masked:
                    s = jnp.where(_diag_mask(tq, tq, 0, 0), s, NEG_INF)
                _online_softmax_step(s, v, m_sc.at[n], l_sc.at[n], acc_sc.at[n])

    def body(j, carry):
        tile(j, False)
        return carry

    lax.fori_loop(0, i, body, 0)
    tile(i, True)

    c = c_ref[...]
    lam_init = c[4:5, 0:1]
    lam = (jnp.exp(jnp.sum(c[0:1] * c[1:2], axis=-1, keepdims=True))
           - jnp.exp(jnp.sum(c[2:3] * c[3:4], axis=-1, keepdims=True)) + lam_init)
    for h in range(C_HEADS):
        o = (_softmax_finish(l_sc.at[2 * h], acc_sc.at[2 * h])
             - lam * _softmax_finish(l_sc.at[2 * h + 1], acc_sc.at[2 * h + 1]))
        o_ref[:, h * C_V:(h + 1) * C_V] = _rms(o, c[5:6]) * (1.0 - lam_init)


def _diff_attn(q, k, v, consts):
    b, s, _ = q.shape
    tq = min(s, 512)
    n_chain = 2 * C_HEADS
    stat = pltpu.VMEM((n_chain, tq, LANES), jnp.float32)
    resident = lambda width: pl.BlockSpec((None, s, width), lambda bi, i: (bi, 0, 0),
                                          pipeline_mode=pl.Buffered(1))
    return pl.pallas_call(
        functools.partial(_diff_attn_kernel, tq=tq),
        grid=(b, s // tq),
        in_specs=[pl.BlockSpec((None, tq, C_WIDTH), lambda bi, i: (bi, i, 0)),
                  resident(C_WIDTH), resident(C_WIDTH),
                  pl.BlockSpec((8, LANES), lambda bi, i: (0, 0))],
        out_specs=pl.BlockSpec((None, tq, C_WIDTH), lambda bi, i: (bi, i, 0)),
        out_shape=jax.ShapeDtypeStruct((b, s, C_WIDTH), jnp.float32),
        scratch_shapes=[pltpu.VMEM((n_chain, tq, 2 * C_QK), MXU_DTYPE), stat, stat,
                        pltpu.VMEM((n_chain, tq, C_V), jnp.float32)],
        compiler_params=_params(2),
        name="diff_attn",
    )(q, k, v, consts)


SUBLANES = 8
COUNT_CHAINS = 4
PACKED_ROWS = 16
HALF_MIN = -2 ** 15
IDX_GROUP = 4


def _tree_sum(parts):
    while len(parts) > 1:
        parts = [a + b for a, b in zip(parts[0::2], parts[1::2])] + parts[len(parts) & ~1:]
    return parts[0]


def _dsa_kernel(qb_ref, qi_ref, w_ref, ki_ref, kb_ref, vb_ref, o_ref,
                keys_sc, half_sc, qz_sc, m_sc, l_sc, acc_sc, *, tq, n_sel):
    i = pl.program_id(1)
    tk = tq
    n_tiles = i + 1
    f32 = jnp.float32

    lane = lax.broadcasted_iota(jnp.int32, (tq, LANES), 1)
    for h in range(IDX_HEADS):
        pair = qi_ref[:, (h // 2) * LANES:(h // 2 + 1) * LANES]
        keep = (lane < IDX_DIM) if h % 2 == 0 else (lane >= IDX_DIM)
        qz_sc[h * tq:(h + 1) * tq, :] = jnp.where(keep, pair, jnp.zeros_like(pair))
    w_t = w_ref[...].T

    def score_tile(j, masked):
        rows = pl.ds(pl.multiple_of(j * tk, tk), tk)
        ki = ki_ref[rows, :]
        sc = jnp.zeros((tk, tq), f32)
        for h0 in range(0, IDX_HEADS, IDX_GROUP):
            rel = _dot_nt(ki, qz_sc[h0 * tq:(h0 + IDX_GROUP) * tq, :])
            for g in range(IDX_GROUP):
                sc = sc + jnp.maximum(rel[:, g * tq:(g + 1) * tq], 0.0) * w_t[h0 + g:h0 + g + 1, :]
        if masked:
            kc = (j * tk + lax.broadcasted_iota(jnp.int32, (tk, tq), 0)) // CHUNK
            qc = (i * tq + lax.broadcasted_iota(jnp.int32, (tk, tq), 1)) // CHUNK
            sc = jnp.where(kc <= qc, sc, NEG_INF)
        bits = lax.bitcast_convert_type(sc, jnp.int32)
        key = jnp.where(bits < 0, INT_MIN - bits, bits)
        key = jnp.where(sc > 0.5 * NEG_INF, key, INT_MIN)
        keys_sc[j] = key
        half_sc[j] = (key >> 16).astype(jnp.int16)

    def score_body(j, carry):
        score_tile(j, False)
        return carry

    lax.fori_loop(0, n_tiles - 1, score_body, 0)
    score_tile(n_tiles - 1, True)

    def count_ge(t):
        def body(j, accs):
            accs = list(accs)
            for n, r in enumerate(range(0, tk, SUBLANES)):
                hit = jnp.where(keys_sc[j, r:r + SUBLANES, :] >= t, 1.0, 0.0)
                accs[n % COUNT_CHAINS] = accs[n % COUNT_CHAINS] + hit
            return tuple(accs)
        zero = jnp.zeros((SUBLANES, tq), f32)
        accs = lax.fori_loop(0, n_tiles, body, (zero,) * COUNT_CHAINS)
        return jnp.sum(_tree_sum(list(accs)), axis=0, keepdims=True)

    def count16(ref, t, strict):
        t16 = t.astype(jnp.int16)
        one, zero = jnp.ones((), jnp.int16), jnp.zeros((), jnp.int16)

        def body(j, accs):
            accs = list(accs)
            for n, r in enumerate(range(0, tk, PACKED_ROWS)):
                x = ref[j, r:r + PACKED_ROWS, :]
                hit = jnp.where((x > t16) if strict else (x >= t16), one, zero)
                accs[n % COUNT_CHAINS] = accs[n % COUNT_CHAINS] + hit
            return tuple(accs)

        accs = lax.fori_loop(0, n_tiles, body, (jnp.zeros((PACKED_ROWS, tq), jnp.int16),) * COUNT_CHAINS)
        return jnp.sum(_tree_sum(list(accs)).astype(jnp.int32), axis=0, keepdims=True)

    def kth_largest16(ref, k_need):
        def step(it, t):
            cand = t + lax.shift_left(jnp.int32(1), jnp.int32(15) - it)
            return jnp.where(count16(ref, cand, False) >= k_need, cand, t)
        return lax.fori_loop(0, 16, step, jnp.full((1, tq), HALF_MIN, jnp.int32))

    thr_hi = kth_largest16(half_sc, n_sel)
    k_lo = n_sel - count16(half_sc, thr_hi, True)
    thr_hi16 = thr_hi.astype(jnp.int16)

    def lo_body(j, carry):
        lo = ((keys_sc[j] & 0xFFFF) + HALF_MIN).astype(jnp.int16)
        half_sc[j] = jnp.where(half_sc[j] == thr_hi16, lo, jnp.int16(HALF_MIN))
        return carry

    lax.fori_loop(0, n_tiles, lo_body, 0)
    thr_lo = kth_largest16(half_sc, k_lo)
    thr = thr_hi * 65536 + (thr_lo - HALF_MIN)
    thr = jnp.maximum(thr, INT_MIN + 1)
    k_f = float(n_sel)
    cnt_ge = count_ge(thr)

    @pl.when(jnp.max(cnt_ge) > k_f)
    def _():
        keep_n = k_f - count_ge(thr + 1)
        tri = (lax.broadcasted_iota(jnp.int32, (tk, tk), 0)
               >= lax.broadcasted_iota(jnp.int32, (tk, tk), 1)).astype(MXU_DTYPE)

        def body(j, carry):
            kt = keys_sc[j]
            eq = kt == thr
            rank = carry + _dot(tri, jnp.where(eq, 1.0, 0.0).astype(MXU_DTYPE))
            keys_sc[j] = jnp.where(eq & (rank > keep_n), INT_MIN, kt)
            return rank[tk - 1:tk, :]

        lax.fori_loop(0, n_tiles, body, jnp.zeros((1, tq), f32))

    def bias_body(j, carry):
        bias = jnp.where(keys_sc[j] >= thr, 0.0, NEG_INF).T
        keys_sc[j] = lax.bitcast_convert_type(bias, jnp.int32)
        return carry

    lax.fori_loop(0, n_tiles, bias_body, 0)
    _softmax_init(m_sc, l_sc, acc_sc)

    def attn_body(j, carry):
        rows = pl.ds(pl.multiple_of(j * tk, tk), tk)
        bias = lax.bitcast_convert_type(keys_sc[j], f32)
        for h in range(B_HEADS):
            sl = slice(h * HEAD_DIM, (h + 1) * HEAD_DIM)
            s = _dot_nt(qb_ref[:, sl], kb_ref[rows, sl]) + bias
            _online_softmax_step(s, vb_ref[rows, sl], m_sc.at[h], l_sc.at[h], acc_sc.at[h])
        return carry

    lax.fori_loop(0, n_tiles, attn_body, 0)
    for h in range(B_HEADS):
        o_ref[:, h * HEAD_DIM:(h + 1) * HEAD_DIM] = _softmax_finish(l_sc.at[h], acc_sc.at[h])


def _dsa_attn(qb, qi, w, ki, kb, vb, n_sel):
    b, s, _ = qb.shape
    tq = min(s, 512)
    blk = lambda width: pl.BlockSpec((None, tq, width), lambda bi, i: (bi, i, 0))
    res = lambda width: pl.BlockSpec((None, s, width), lambda bi, i: (bi, 0, 0), pipeline_mode=pl.Buffered(1))
    return pl.pallas_call(
        functools.partial(_dsa_kernel, tq=tq, n_sel=n_sel),
        grid=(b, s // tq),
        in_specs=[blk(B_WIDTH), blk(IDX_HEADS * IDX_DIM), blk(LANES), res(LANES), res(B_WIDTH), res(B_WIDTH)],
        out_specs=blk(B_WIDTH),
        out_shape=jax.ShapeDtypeStruct((b, s, B_WIDTH), jnp.float32),
        scratch_shapes=[pltpu.VMEM((s // tq, tq, tq), jnp.int32),
                        pltpu.VMEM((s // tq, tq, tq), jnp.int16),
                        pltpu.VMEM((IDX_HEADS * tq, LANES), MXU_DTYPE),
                        pltpu.VMEM((B_HEADS, tq, LANES), jnp.float32),
                        pltpu.VMEM((B_HEADS, tq, LANES), jnp.float32),
                        pltpu.VMEM((B_HEADS, tq, HEAD_DIM), jnp.float32)],
        compiler_params=_params(2),
        name="dsa_attn",
    )(qb, qi, w, ki, kb, vb)


def _out_kernel(oa_ref, ob_ref, oc_ref, g_ref, h_ref, p_ref, wo_ref, wpg_ref, wple_ref, gn_ref,
                *out_refs, emit_h):
    g = g_ref[...]
    sg = g * _sigmoid(g)
    h1 = h_ref[...]
    col = 0
    for o_ref in (oa_ref, ob_ref, oc_ref):
        width = o_ref.shape[1]
        mixed = (o_ref[...] * sg[:, col:col + width]).astype(MXU_DTYPE)
        h1 = h1 + _dot(mixed, wo_ref[col:col + width, :])
        col += width
    gate = _sigmoid(_dot(h1.astype(MXU_DTYPE), wpg_ref[...]))
    h2 = h1 + _dot(p_ref[...].astype(MXU_DTYPE), wple_ref[...]) * gate
    if emit_h:
        out_refs[0][...] = h2
    out_refs[-1][...] = _rms(h2, gn_ref[...]).astype(out_refs[-1].dtype)


def _out_proj(oa, ob, oc, g, h, p, wo, wpg, wple, gn, emit_h):
    n, d = h.shape
    tm = min(n, 256)
    row = lambda a: pl.BlockSpec((tm, a.shape[1]), lambda i: (i, 0))
    const = lambda a: pl.BlockSpec(a.shape, lambda i: (0,) * a.ndim, pipeline_mode=pl.Buffered(1))
    out_specs = [pl.BlockSpec((tm, d), lambda i: (i, 0))]
    out_shape = [jax.ShapeDtypeStruct((n, d), MXU_DTYPE if emit_h else jnp.float32)]
    if emit_h:
        out_specs = [pl.BlockSpec((tm, d), lambda i: (i, 0))] + out_specs
        out_shape = [jax.ShapeDtypeStruct((n, d), jnp.float32)] + out_shape
    return pl.pallas_call(
        functools.partial(_out_kernel, emit_h=emit_h),
        grid=(n // tm,),
        in_specs=[row(oa), row(ob), row(oc), row(g), row(h), row(p), const(wo), const(wpg), const(wple),
                  pl.BlockSpec((1, d), lambda i: (0, 0))],
        out_specs=out_specs,
        out_shape=out_shape,
        compiler_params=_params(1),
        name="out_proj",
    )(oa, ob, oc, g, h, p, wo, wpg, wple, gn)


def _in_cols(name):
    k = IN_NAMES.index(name)
    start = sum(IN_SPLITS[:k])
    return start, start + IN_SPLITS[k]


def _gather_cols(w_in, parts):
    cols = []
    for part in parts:
        if isinstance(part, int):
            cols.append(jnp.zeros(w_in.shape[:2] + (part,), w_in.dtype))
        else:
            a, b = _in_cols(part)
            cols.append(w_in[:, :, a:b])
    return jnp.concatenate(cols, axis=-1).astype(MXU_DTYPE)


_SEG_G = ((0, D_MIX, None, None),)
_SEG_B = ((0, B_WIDTH, PAT_DSA, HEAD_DIM ** -0.5 * LOG2E), (B_WIDTH, B_WIDTH, PAT_DSA, None),
          (2 * B_WIDTH, B_WIDTH, None, None), (3 * B_WIDTH, IDX_HEADS * IDX_DIM, PAT_SMALL, None),
          (3 * B_WIDTH + IDX_HEADS * IDX_DIM, LANES, PAT_SMALL, None),
          (3 * B_WIDTH + IDX_HEADS * IDX_DIM + LANES, LANES, None, None))
_SEG_C = ((0, C_WIDTH, PAT_SMALL, C_QK ** -0.5 * LOG2E), (C_WIDTH, C_WIDTH, PAT_SMALL, None),
          (2 * C_WIDTH, C_WIDTH, None, None))


def kernel(x, p, positions, w_in, w_uq, w_ukv, w_o, norm_g, q_norm_g, kv_norm_g,
           lam_q1, lam_k1, lam_q2, lam_k2, subln_g, w_ple, w_pg, final_g):
    b, s, d = x.shape
    n = b * s
    f32 = jnp.float32
    n_sel = min(TOPK_MAX, s // 4)

    w_a = _gather_cols(w_in, ("c_q", "c_kv", "k_rope", LANES - A_ROPE))
    w_g = _gather_cols(w_in, ("g_a", "g_b", "g_c"))
    w_b = _gather_cols(w_in, ("q_b", "k_b", "v_b", "q_idx", "k_idx", "k_idx", "w_idx", LANES - IDX_HEADS))
    w_c = _gather_cols(w_in, ("q_c", "k_c", "v_c"))
    uq = w_uq.reshape(DEPTH, A_Q_RANK, A_HEADS, A_NOPE + A_ROPE)
    uq = jnp.pad(uq, ((0, 0), (0, 0), (0, 0), (0, A_QK_PAD - A_NOPE - A_ROPE)))
    uq = uq.reshape(DEPTH, A_Q_RANK, A_HEADS * A_QK_PAD).astype(MXU_DTYPE)
    ukv = w_ukv.reshape(DEPTH, A_KV_RANK, A_HEADS, A_NOPE + A_V)
    ukv = jnp.concatenate([ukv[..., :A_NOPE].reshape(DEPTH, A_KV_RANK, A_WIDTH),
                           ukv[..., A_NOPE:].reshape(DEPTH, A_KV_RANK, A_WIDTH)], axis=-1).astype(MXU_DTYPE)
    w_o_c = w_o.astype(MXU_DTYPE)
    w_pg_c = w_pg.astype(MXU_DTYPE)
    w_ple_c = w_ple.astype(MXU_DTYPE)
    pad_lanes = lambda v: jnp.pad(v, (0, LANES - v.shape[0]))

    tabs = _rope_tables(positions.astype(f32).reshape(n, 1))
    h = x.reshape(n, d)
    u = _norm(h, norm_g[0], MXU_DTYPE)
    out = None
    for i in range(DEPTH):
        q_a, k_a, v_a = _mla_proj(u, w_a[i], q_norm_g[i].reshape(1, -1), kv_norm_g[i].reshape(1, -1),
                                  uq[i], ukv[i], tabs)
        (g,) = _proj(u, w_g[i], None, _SEG_G, (f32,), "proj_gates")
        q_b, k_b, v_b, q_i, k_i, w_i = _proj(u, w_b[i], tabs, _SEG_B,
                                             (MXU_DTYPE,) * 5 + (f32,), "proj_dsa")
        q_c, k_c, v_c = _proj(u, w_c[i], tabs, _SEG_C, (MXU_DTYPE,) * 3, "proj_diff")

        r3 = lambda a: a.reshape(b, s, a.shape[-1])
        o_a = _mla_attn(r3(q_a), r3(k_a), r3(v_a))
        o_b = _dsa_attn(r3(q_b), r3(q_i), r3(w_i), r3(k_i), r3(k_b), r3(v_b), n_sel)
        lam_init = 0.8 - 0.6 * math.exp(-0.3 * i)
        consts = jnp.stack([pad_lanes(lam_q1[i]), pad_lanes(lam_k1[i]), pad_lanes(lam_q2[i]),
                            pad_lanes(lam_k2[i]), jnp.full((LANES,), lam_init, f32), subln_g[i],
                            jnp.zeros((LANES,), f32), jnp.zeros((LANES,), f32)]).astype(f32)
        o_c = _diff_attn(r3(q_c), r3(k_c), r3(v_c), consts)

        last = i == DEPTH - 1
        gn = (final_g if last else norm_g[i + 1]).reshape(1, d)
        res = _out_proj(o_a.reshape(n, -1), o_b.reshape(n, -1), o_c.reshape(n, -1), g, h,
                        p[i].reshape(n, PLE_DIM), w_o_c[i], w_pg_c[i], w_ple_c[i], gn, emit_h=not last)
        if last:
            out = res[0]
        else:
            h, u = res
    return out.reshape(b, s, d)
```

```python
import functools
import math

import jax
import jax.numpy as jnp
import numpy as np
from jax import lax
from jax.experimental import pallas as pl
from jax.experimental.pallas import tpu as pltpu

D_MODEL = 2048
DEPTH = 4
CHUNK = 64
ROPE_THETA = 500000.0
NORM_EPS = 1e-6
NEG_INF = -1e30
PLE_DIM = 256
HEAD_DIM = 128

A_HEADS = 6
A_Q_RANK = 384
A_KV_RANK = 256
A_NOPE = 128
A_ROPE = 64
A_V = 128
A_WIDTH = A_HEADS * A_V
B_HEADS = 5
B_WIDTH = B_HEADS * HEAD_DIM
IDX_HEADS = 8
IDX_DIM = 64
TOPK_MAX = 256
C_HEADS = 5
C_QK = 64
C_V = 128
C_WIDTH = C_HEADS * C_V
D_MIX = A_WIDTH + B_WIDTH + C_WIDTH

IN_NAMES = ("c_q", "c_kv", "k_rope", "g_a", "q_b", "k_b", "v_b", "q_idx", "k_idx", "w_idx", "g_b",
            "q_c", "k_c", "v_c", "g_c")
IN_SPLITS = (A_Q_RANK, A_KV_RANK, A_ROPE, A_WIDTH,
             B_WIDTH, B_WIDTH, B_WIDTH, IDX_HEADS * IDX_DIM, IDX_DIM, IDX_HEADS, B_WIDTH,
             C_HEADS * 2 * C_QK, C_HEADS * 2 * C_QK, C_WIDTH, C_WIDTH)

LANES = 128
A_QK_PAD = 256
INT_MIN = -2 ** 31
LOG2E = math.log2(math.e)
MXU_DTYPE = jnp.bfloat16
VMEM_LIMIT = 56 * 1024 * 1024

ROPE_PATTERNS = ((A_ROPE, 64), (HEAD_DIM // 4, 128), (IDX_DIM // 4, 64))
PAT_MLA, PAT_DSA, PAT_SMALL = 0, 1, 2

_NT = (((1,), (1,)), ((), ()))


def _params(n_axes):
    return pltpu.CompilerParams(dimension_semantics=("arbitrary",) * n_axes,
                                vmem_limit_bytes=VMEM_LIMIT)


def _dot(a, b):
    return jnp.dot(a, b, preferred_element_type=jnp.float32)


def _dot_nt(a, b):
    return lax.dot_general(a, b, _NT, preferred_element_type=jnp.float32)


def _sigmoid(x):
    return 1.0 / (1.0 + jnp.exp(-x))


def _rms(x, g):
    return x * lax.rsqrt(jnp.mean(x * x, axis=-1, keepdims=True) + NORM_EPS) * g


def _rope_pattern_rows():
    freq = []
    m_up = np.zeros((3, LANES), np.float32)
    m_dn = np.zeros((3, LANES), np.float32)
    for p, (n_rot, period) in enumerate(ROPE_PATTERNS):
        half = n_rot // 2
        inv = 1.0 / (ROPE_THETA ** (jnp.arange(half, dtype=jnp.float32) * (2.0 / n_rot)))
        one = jnp.concatenate([inv, inv, jnp.zeros((period - n_rot,), jnp.float32)])
        freq.append(jnp.tile(one, LANES // period))
        for lane in range(LANES):
            m = lane % period
            m_up[p, lane] = float(m < half)
            m_dn[p, lane] = float(half <= m < n_rot)
    return jnp.stack(freq), jnp.asarray(m_up), jnp.asarray(m_dn)


def _rope_tab_kernel(pos_ref, f_ref, mu_ref, md_ref, c_ref, sa_ref, sb_ref):
    pos = pos_ref[...]
    for p in range(3):
        ang = pos * f_ref[p:p + 1, :]
        sin = jnp.sin(ang)
        c_ref[p] = jnp.cos(ang)
        sa_ref[p] = -sin * mu_ref[p:p + 1, :]
        sb_ref[p] = sin * md_ref[p:p + 1, :]


def _rope_tables(pos_f32):
    n = pos_f32.shape[0]
    tm = min(n, 1024)
    freq, m_up, m_dn = _rope_pattern_rows()
    row = pl.BlockSpec((3, LANES), lambda i: (0, 0))
    tab = pl.BlockSpec((3, tm, LANES), lambda i: (0, i, 0))
    shape = jax.ShapeDtypeStruct((3, n, LANES), jnp.float32)
    return pl.pallas_call(
        _rope_tab_kernel,
        grid=(n // tm,),
        in_specs=[pl.BlockSpec((tm, 1), lambda i: (i, 0)), row, row, row],
        out_specs=[tab, tab, tab],
        out_shape=[shape, shape, shape],
        compiler_params=_params(1),
        name="rope_tables",
    )(pos_f32, freq, m_up, m_dn)


def _rope128(x, c, sa, sb, half):
    return x * c + pltpu.roll(x, LANES - half, 1) * sa + pltpu.roll(x, half, 1) * sb


def _norm_kernel(x_ref, g_ref, o_ref):
    o_ref[...] = _rms(x_ref[...], g_ref[...]).astype(o_ref.dtype)


def _norm(x, g, out_dtype):
    n, d = x.shape
    tm = min(n, 512)
    return pl.pallas_call(
        _norm_kernel,
        grid=(n // tm,),
        in_specs=[pl.BlockSpec((tm, d), lambda i: (i, 0)), pl.BlockSpec((1, d), lambda i: (0, 0))],
        out_specs=pl.BlockSpec((tm, d), lambda i: (i, 0)),
        out_shape=jax.ShapeDtypeStruct((n, d), out_dtype),
        compiler_params=_params(1),
        name="rms_norm",
    )(x, g.reshape(1, d))


def _proj_kernel(*refs, segs, with_tables):
    u_ref, w_ref = refs[0], refs[1]
    if with_tables:
        c_ref, sa_ref, sb_ref = refs[2:5]
        outs = refs[5:]
    else:
        outs = refs[2:]
    u = u_ref[...]
    for (c0, width, pat, scale), o_ref in zip(segs, outs):
        z = _dot(u, w_ref[:, c0:c0 + width])
        if pat is None and scale is None:
            o_ref[...] = z.astype(o_ref.dtype)
            continue
        for ch in range(width // LANES):
            sl = slice(ch * LANES, (ch + 1) * LANES)
            x = z[:, sl]
            if pat is not None:
                x = _rope128(x, c_ref[pat], sa_ref[pat], sb_ref[pat], ROPE_PATTERNS[pat][0] // 2)
            if scale is not None:
                x = x * scale
            o_ref[:, sl] = x.astype(o_ref.dtype)


def _proj(u, w, tabs, segs, out_dtypes, name):
    n, d = u.shape
    tm = min(n, 512)
    width = w.shape[1]
    in_specs = [pl.BlockSpec((tm, d), lambda i: (i, 0)), pl.BlockSpec((d, width), lambda i: (0, 0))]
    args = [u, w]
    if tabs is not None:
        in_specs += [pl.BlockSpec((3, tm, LANES), lambda i: (0, i, 0))] * 3
        args += list(tabs)
    out_specs = [pl.BlockSpec((tm, s[1]), lambda i: (i, 0)) for s in segs]
    out_shape = [jax.ShapeDtypeStruct((n, s[1]), dt) for s, dt in zip(segs, out_dtypes)]
    return pl.pallas_call(
        functools.partial(_proj_kernel, segs=tuple(segs), with_tables=tabs is not None),
        grid=(n // tm,),
        in_specs=in_specs,
        out_specs=out_specs,
        out_shape=out_shape,
        compiler_params=_params(1),
        name=name,
    )(*args)


def _mla_proj_kernel(u_ref, w_ref, qg_ref, kvg_ref, wuq_ref, wukv_ref, c_ref, sa_ref, sb_ref,
                     q_ref, k_ref, v_ref):
    z = _dot(u_ref[...], w_ref[...])
    c, sa, sb = c_ref[PAT_MLA], sa_ref[PAT_MLA], sb_ref[PAT_MLA]
    half = A_ROPE // 2
    scale = (A_NOPE + A_ROPE) ** -0.5 * LOG2E
    c_q = z[:, :A_Q_RANK]
    c_kv = z[:, A_Q_RANK:A_Q_RANK + A_KV_RANK]
    k_rope = z[:, A_Q_RANK + A_KV_RANK:]
    q = _dot(_rms(c_q, qg_ref[...]).astype(MXU_DTYPE), wuq_ref[...])
    kv = _dot(_rms(c_kv, kvg_ref[...]).astype(MXU_DTYPE), wukv_ref[...])
    k_pe = _rope128(k_rope, c, sa, sb, half).astype(k_ref.dtype)
    for h in range(A_HEADS):
        o = h * A_QK_PAD
        q_ref[:, o:o + A_NOPE] = (q[:, o:o + A_NOPE] * scale).astype(q_ref.dtype)
        q_rot = _rope128(q[:, o + A_NOPE:o + A_QK_PAD], c, sa, sb, half)
        q_ref[:, o + A_NOPE:o + A_QK_PAD] = (q_rot * scale).astype(q_ref.dtype)
        k_ref[:, o:o + A_NOPE] = kv[:, h * A_NOPE:(h + 1) * A_NOPE].astype(k_ref.dtype)
        k_ref[:, o + A_NOPE:o + A_QK_PAD] = k_pe
    v_ref[...] = kv[:, A_WIDTH:].astype(v_ref.dtype)


def _mla_proj(u, w, qg, kvg, wuq, wukv, tabs):
    n, d = u.shape
    tm = min(n, 512)
    full = lambda a: pl.BlockSpec(a.shape, lambda i: (0,) * a.ndim)
    tab = pl.BlockSpec((3, tm, LANES), lambda i: (0, i, 0))
    qk = A_HEADS * A_QK_PAD
    return pl.pallas_call(
        _mla_proj_kernel,
        grid=(n // tm,),
        in_specs=[pl.BlockSpec((tm, d), lambda i: (i, 0)), full(w), full(qg), full(kvg), full(wuq),
                  full(wukv), tab, tab, tab],
        out_specs=[pl.BlockSpec((tm, qk), lambda i: (i, 0)), pl.BlockSpec((tm, qk), lambda i: (i, 0)),
                   pl.BlockSpec((tm, A_WIDTH), lambda i: (i, 0))],
        out_shape=[jax.ShapeDtypeStruct((n, qk), MXU_DTYPE), jax.ShapeDtypeStruct((n, qk), MXU_DTYPE),
                   jax.ShapeDtypeStruct((n, A_WIDTH), MXU_DTYPE)],
        compiler_params=_params(1),
        name="mla_proj",
    )(u, w, qg, kvg, wuq, wukv, *tabs)


def _diag_mask(tq, tk, q0, k0):
    qc = (q0 + lax.broadcasted_iota(jnp.int32, (tq, tk), 0)) // CHUNK
    kc = (k0 + lax.broadcasted_iota(jnp.int32, (tq, tk), 1)) // CHUNK
    return kc <= qc


def _softmax_init(m_ref, l_ref, acc_ref):
    m_ref[...] = jnp.full(m_ref.shape, NEG_INF, jnp.float32)
    l_ref[...] = jnp.zeros(l_ref.shape, jnp.float32)
    acc_ref[...] = jnp.zeros(acc_ref.shape, jnp.float32)


def _online_softmax_step(s, v, m_ref, l_ref, acc_ref):
    chunks = [s[:, c * LANES:(c + 1) * LANES] for c in range(s.shape[1] // LANES)]
    cmax = functools.reduce(jnp.maximum, chunks)
    m_prev = m_ref[...]
    m_new = jnp.maximum(m_prev, jnp.max(cmax, axis=-1, keepdims=True))
    alpha = jnp.exp2(m_prev - m_new)
    ps = [jnp.exp2(c - m_new) for c in chunks]
    l_ref[...] = alpha * l_ref[...] + functools.reduce(jnp.add, ps)
    p = jnp.concatenate([x.astype(v.dtype) for x in ps], axis=1)
    acc_ref[...] = alpha * acc_ref[...] + _dot(p, v)
    m_ref[...] = m_new


def _softmax_finish(l_ref, acc_ref):
    return acc_ref[...] / jnp.sum(l_ref[...], axis=-1, keepdims=True)


MLA_HEADS_PER_STEP = 6


def _mla_attn_kernel(q_ref, k_ref, v_ref, o_ref, m_sc, l_sc, acc_sc, *, tq):
    i = pl.program_id(2)
    _softmax_init(m_sc, l_sc, acc_sc)

    def tile(j, masked):
        rows = pl.ds(pl.multiple_of(j * tq, tq), tq)
        for h in range(MLA_HEADS_PER_STEP):
            qk = slice(h * A_QK_PAD, (h + 1) * A_QK_PAD)
            s = _dot_nt(q_ref[:, qk], k_ref[rows, qk])
            if masked:
                s = jnp.where(_diag_mask(tq, tq, 0, 0), s, NEG_INF)
            _online_softmax_step(s, v_ref[rows, h * A_V:(h + 1) * A_V], m_sc.at[h], l_sc.at[h], acc_sc.at[h])

    def body(j, carry):
        tile(j, False)
        return carry

    lax.fori_loop(0, i, body, 0)
    tile(i, True)
    for h in range(MLA_HEADS_PER_STEP):
        o_ref[:, h * A_V:(h + 1) * A_V] = _softmax_finish(l_sc.at[h], acc_sc.at[h])


def _mla_attn(q, k, v):
    b, s, _ = q.shape
    tq = min(s, 512)
    hp = MLA_HEADS_PER_STEP
    stat = pltpu.VMEM((hp, tq, LANES), jnp.float32)
    return pl.pallas_call(
        functools.partial(_mla_attn_kernel, tq=tq),
        grid=(b, A_HEADS // hp, s // tq),
        in_specs=[pl.BlockSpec((None, tq, hp * A_QK_PAD), lambda bi, h, i: (bi, i, h)),
                  pl.BlockSpec((None, s, hp * A_QK_PAD), lambda bi, h, i: (bi, 0, h),
                               pipeline_mode=pl.Buffered(1)),
                  pl.BlockSpec((None, s, hp * A_V), lambda bi, h, i: (bi, 0, h),
                               pipeline_mode=pl.Buffered(1))],
        out_specs=pl.BlockSpec((None, tq, hp * A_V), lambda bi, h, i: (bi, i, h)),
        out_shape=jax.ShapeDtypeStruct((b, s, A_WIDTH), jnp.float32),
        scratch_shapes=[stat, stat, pltpu.VMEM((hp, tq, A_V), jnp.float32)],
        compiler_params=_params(3),
        name="mla_attn",
    )(q, k, v)


def _diff_attn_kernel(q_ref, k_ref, v_ref, c_ref, o_ref, qs_sc, m_sc, l_sc, acc_sc, *, tq):
    i = pl.program_id(1)
    lane = lax.broadcasted_iota(jnp.int32, (tq, 2 * C_QK), 1)
    for h in range(C_HEADS):
        q = q_ref[:, h * 2 * C_QK:(h + 1) * 2 * C_QK]
        qs_sc[2 * h] = jnp.where(lane < C_QK, q, jnp.zeros_like(q))
        qs_sc[2 * h + 1] = jnp.where(lane >= C_QK, q, jnp.zeros_like(q))
    _softmax_init(m_sc, l_sc, acc_sc)

    def tile(j, masked):
        rows = pl.ds(pl.multiple_of(j * tq, tq), tq)
        for h in range(C_HEADS):
            k = k_ref[rows, h * 2 * C_QK:(h + 1) * 2 * C_QK]
            v = v_ref[rows, h * C_V:(h + 1) * C_V]
            for t in range(2):
                n = 2 * h + t
                s = _dot_nt(qs_sc[n], k)
                if masked:
                    s = jnp.where(_diag_mask(tq, tq, 0, 0), s, NEG_INF)
                _online_softmax_step(s, v, m_sc.at[n], l_sc.at[n], acc_sc.at[n])

    def body(j, carry):
        tile(j, False)
        return carry

    lax.fori_loop(0, i, body, 0)
    tile(i, True)

    c = c_ref[...]
    lam_init = c[4:5, 0:1]
    lam = (jnp.exp(jnp.sum(c[0:1] * c[1:2], axis=-1, keepdims=True))
           - jnp.exp(jnp.sum(c[2:3] * c[3:4], axis=-1, keepdims=True)) + lam_init)
    for h in range(C_HEADS):
        o = (_softmax_finish(l_sc.at[2 * h], acc_sc.at[2 * h])
             - lam * _softmax_finish(l_sc.at[2 * h + 1], acc_sc.at[2 * h + 1]))
        o_ref[:, h * C_V:(h + 1) * C_V] = _rms(o, c[5:6]) * (1.0 - lam_init)


def _diff_attn(q, k, v, consts):
    b, s, _ = q.shape
    tq = min(s, 512)
    n_chain = 2 * C_HEADS
    stat = pltpu.VMEM((n_chain, tq, LANES), jnp.float32)
    resident = lambda width: pl.BlockSpec((None, s, width), lambda bi, i: (bi, 0, 0),
                                          pipeline_mode=pl.Buffered(1))
    return pl.pallas_call(
        functools.partial(_diff_attn_kernel, tq=tq),
        grid=(b, s // tq),
        in_specs=[pl.BlockSpec((None, tq, C_WIDTH), lambda bi, i: (bi, i, 0)),
                  resident(C_WIDTH), resident(C_WIDTH),
                  pl.BlockSpec((8, LANES), lambda bi, i: (0, 0))],
        out_specs=pl.BlockSpec((None, tq, C_WIDTH), lambda bi, i: (bi, i, 0)),
        out_shape=jax.ShapeDtypeStruct((b, s, C_WIDTH), jnp.float32),
        scratch_shapes=[pltpu.VMEM((n_chain, tq, 2 * C_QK), MXU_DTYPE), stat, stat,
                        pltpu.VMEM((n_chain, tq, C_V), jnp.float32)],
        compiler_params=_params(2),
        name="diff_attn",
    )(q, k, v, consts)


SUBLANES = 8
COUNT_CHAINS = 4
PACKED_ROWS = 16
HALF_MIN = -2 ** 15
IDX_GROUP = 4


def _tree_sum(parts):
    while len(parts) > 1:
        parts = [a + b for a, b in zip(parts[0::2], parts[1::2])] + parts[len(parts) & ~1:]
    return parts[0]


def _dsa_kernel(qb_ref, qi_ref, w_ref, ki_ref, kb_ref, vb_ref, o_ref,
                keys_sc, half_sc, qz_sc, m_sc, l_sc, acc_sc, *, tq, n_sel):
    i = pl.program_id(1)
    tk = tq
    n_tiles = i + 1
    f32 = jnp.float32

    lane = lax.broadcasted_iota(jnp.int32, (tq, LANES), 1)
    for h in range(IDX_HEADS):
        pair = qi_ref[:, (h // 2) * LANES:(h // 2 + 1) * LANES]
        keep = (lane < IDX_DIM) if h % 2 == 0 else (lane >= IDX_DIM)
        qz_sc[h * tq:(h + 1) * tq, :] = jnp.where(keep, pair, jnp.zeros_like(pair))
    w_t = w_ref[...].T

    def score_tile(j, masked):
        rows = pl.ds(pl.multiple_of(j * tk, tk), tk)
        ki = ki_ref[rows, :]
        sc = jnp.zeros((tk, tq), f32)
        for h0 in range(0, IDX_HEADS, IDX_GROUP):
            rel = _dot_nt(ki, qz_sc[h0 * tq:(h0 + IDX_GROUP) * tq, :])
            for g in range(IDX_GROUP):
                sc = sc + jnp.maximum(rel[:, g * tq:(g + 1) * tq], 0.0) * w_t[h0 + g:h0 + g + 1, :]
        if masked:
            kc = (j * tk + lax.broadcasted_iota(jnp.int32, (tk, tq), 0)) // CHUNK
            qc = (i * tq + lax.broadcasted_iota(jnp.int32, (tk, tq), 1)) // CHUNK
            sc = jnp.where(kc <= qc, sc, NEG_INF)
        bits = lax.bitcast_convert_type(sc, jnp.int32)
        key = jnp.where(bits < 0, INT_MIN - bits, bits)
        key = jnp.where(sc > 0.5 * NEG_INF, key, INT_MIN)
        keys_sc[j] = key
        half_sc[j] = (key >> 16).astype(jnp.int16)

    def score_body(j, carry):
        score_tile(j, False)
        return carry

    lax.fori_loop(0, n_tiles - 1, score_body, 0)
    score_tile(n_tiles - 1, True)

    def count_ge(t):
        def body(j, accs):
            accs = list(accs)
            for n, r in enumerate(range(0, tk, SUBLANES)):
                hit = jnp.where(keys_sc[j, r:r + SUBLANES, :] >= t, 1.0, 0.0)
                accs[n % COUNT_CHAINS] = accs[n % COUNT_CHAINS] + hit
            return tuple(accs)
        zero = jnp.zeros((SUBLANES, tq), f32)
        accs = lax.fori_loop(0, n_tiles, body, (zero,) * COUNT_CHAINS)
        return jnp.sum(_tree_sum(list(accs)), axis=0, keepdims=True)

    def count16(ref, t, strict):
        t16 = t.astype(jnp.int16)
        one, zero = jnp.ones((), jnp.int16), jnp.zeros((), jnp.int16)

        def body(j, accs):
            accs = list(accs)
            for n, r in enumerate(range(0, tk, PACKED_ROWS)):
                x = ref[j, r:r + PACKED_ROWS, :]
                hit = jnp.where((x > t16) if strict else (x >= t16), one, zero)
                accs[n % COUNT_CHAINS] = accs[n % COUNT_CHAINS] + hit
            return tuple(accs)

        accs = lax.fori_loop(0, n_tiles, body, (jnp.zeros((PACKED_ROWS, tq), jnp.int16),) * COUNT_CHAINS)
        return jnp.sum(_tree_sum(list(accs)).astype(jnp.int32), axis=0, keepdims=True)

    def kth_largest16(ref, k_need):
        def step(it, t):
            cand = t + lax.shift_left(jnp.int32(1), jnp.int32(15) - it)
            return jnp.where(count16(ref, cand, False) >= k_need, cand, t)
        return lax.fori_loop(0, 16, step, jnp.full((1, tq), HALF_MIN, jnp.int32))

    thr_hi = kth_largest16(half_sc, n_sel)
    k_lo = n_sel - count16(half_sc, thr_hi, True)
    thr_hi16 = thr_hi.astype(jnp.int16)

    def lo_body(j, carry):
        lo = ((keys_sc[j] & 0xFFFF) + HALF_MIN).astype(jnp.int16)
        half_sc[j] = jnp.where(half_sc[j] == thr_hi16, lo, jnp.int16(HALF_MIN))
        return carry

    lax.fori_loop(0, n_tiles, lo_body, 0)
    thr_lo = kth_largest16(half_sc, k_lo)
    thr = thr_hi * 65536 + (thr_lo - HALF_MIN)
    thr = jnp.maximum(thr, INT_MIN + 1)
    k_f = float(n_sel)
    cnt_ge = count_ge(thr)

    @pl.when(jnp.max(cnt_ge) > k_f)
    def _():
        keep_n = k_f - count_ge(thr + 1)
        tri = (lax.broadcasted_iota(jnp.int32, (tk, tk), 0)
               >= lax.broadcasted_iota(jnp.int32, (tk, tk), 1)).astype(MXU_DTYPE)

        def body(j, carry):
            kt = keys_sc[j]
            eq = kt == thr
            rank = carry + _dot(tri, jnp.where(eq, 1.0, 0.0).astype(MXU_DTYPE))
            keys_sc[j] = jnp.where(eq & (rank > keep_n), INT_MIN, kt)
            return rank[tk - 1:tk, :]

        lax.fori_loop(0, n_tiles, body, jnp.zeros((1, tq), f32))

    def bias_body(j, carry):
        bias = jnp.where(keys_sc[j] >= thr, 0.0, NEG_INF).T
        keys_sc[j] = lax.bitcast_convert_type(bias, jnp.int32)
        return carry

    lax.fori_loop(0, n_tiles, bias_body, 0)
    _softmax_init(m_sc, l_sc, acc_sc)

    def attn_body(j, carry):
        rows = pl.ds(pl.multiple_of(j * tk, tk), tk)
        bias = lax.bitcast_convert_type(keys_sc[j], f32)
        for h in range(B_HEADS):
            sl = slice(h * HEAD_DIM, (h + 1) * HEAD_DIM)
            s = _dot_nt(qb_ref[:, sl], kb_ref[rows, sl]) + bias
            _online_softmax_step(s, vb_ref[rows, sl], m_sc.at[h], l_sc.at[h], acc_sc.at[h])
        return carry

    lax.fori_loop(0, n_tiles, attn_body, 0)
    for h in range(B_HEADS):
        o_ref[:, h * HEAD_DIM:(h + 1) * HEAD_DIM] = _softmax_finish(l_sc.at[h], acc_sc.at[h])


def _dsa_attn(qb, qi, w, ki, kb, vb, n_sel):
    b, s, _ = qb.shape
    tq = min(s, 512)
    blk = lambda width: pl.BlockSpec((None, tq, width), lambda bi, i: (bi, i, 0))
    res = lambda width: pl.BlockSpec((None, s, width), lambda bi, i: (bi, 0, 0), pipeline_mode=pl.Buffered(1))
    return pl.pallas_call(
        functools.partial(_dsa_kernel, tq=tq, n_sel=n_sel),
        grid=(b, s // tq),
        in_specs=[blk(B_WIDTH), blk(IDX_HEADS * IDX_DIM), blk(LANES), res(LANES), res(B_WIDTH), res(B_WIDTH)],
        out_specs=blk(B_WIDTH),
        out_shape=jax.ShapeDtypeStruct((b, s, B_WIDTH), jnp.float32),
        scratch_shapes=[pltpu.VMEM((s // tq, tq, tq), jnp.int32),
                        pltpu.VMEM((s // tq, tq, tq), jnp.int16),
                        pltpu.VMEM((IDX_HEADS * tq, LANES), MXU_DTYPE),
                        pltpu.VMEM((B_HEADS, tq, LANES), jnp.float32),
                        pltpu.VMEM((B_HEADS, tq, LANES), jnp.float32),
                        pltpu.VMEM((B_HEADS, tq, HEAD_DIM), jnp.float32)],
        compiler_params=_params(2),
        name="dsa_attn",
    )(qb, qi, w, ki, kb, vb)


def _out_kernel(oa_ref, ob_ref, oc_ref, g_ref, h_ref, p_ref, wo_ref, wpg_ref, wple_ref, gn_ref,
                *out_refs, emit_h):
    g = g_ref[...]
    sg = g * _sigmoid(g)
    h1 = h_ref[...]
    col = 0
    for o_ref in (oa_ref, ob_ref, oc_ref):
        width = o_ref.shape[1]
        mixed = (o_ref[...] * sg[:, col:col + width]).astype(MXU_DTYPE)
        h1 = h1 + _dot(mixed, wo_ref[col:col + width, :])
        col += width
    gate = _sigmoid(_dot(h1.astype(MXU_DTYPE), wpg_ref[...]))
    h2 = h1 + _dot(p_ref[...].astype(MXU_DTYPE), wple_ref[...]) * gate
    if emit_h:
        out_refs[0][...] = h2
    out_refs[-1][...] = _rms(h2, gn_ref[...]).astype(out_refs[-1].dtype)


def _out_proj(oa, ob, oc, g, h, p, wo, wpg, wple, gn, emit_h):
    n, d = h.shape
    tm = min(n, 256)
    row = lambda a: pl.BlockSpec((tm, a.shape[1]), lambda i: (i, 0))
    const = lambda a: pl.BlockSpec(a.shape, lambda i: (0,) * a.ndim, pipeline_mode=pl.Buffered(1))
    out_specs = [pl.BlockSpec((tm, d), lambda i: (i, 0))]
    out_shape = [jax.ShapeDtypeStruct((n, d), MXU_DTYPE if emit_h else jnp.float32)]
    if emit_h:
        out_specs = [pl.BlockSpec((tm, d), lambda i: (i, 0))] + out_specs
        out_shape = [jax.ShapeDtypeStruct((n, d), jnp.float32)] + out_shape
    return pl.pallas_call(
        functools.partial(_out_kernel, emit_h=emit_h),
        grid=(n // tm,),
        in_specs=[row(oa), row(ob), row(oc), row(g), row(h), row(p), const(wo), const(wpg), const(wple),
                  pl.BlockSpec((1, d), lambda i: (0, 0))],
        out_specs=out_specs,
        out_shape=out_shape,
        compiler_params=_params(1),
        name="out_proj",
    )(oa, ob, oc, g, h, p, wo, wpg, wple, gn)


def _in_cols(name):
    k = IN_NAMES.index(name)
    start = sum(IN_SPLITS[:k])
    return start, start + IN_SPLITS[k]


_W_BLOCKS = (("c_q", "c_kv", "k_rope", LANES - A_ROPE),
             ("g_a", "g_b", "g_c"),
             ("q_b", "k_b", "v_b", "q_idx", "k_idx", "k_idx", "w_idx", LANES - IDX_HEADS),
             ("q_c", "k_c", "v_c"))


def _block_width(parts):
    return sum(p if isinstance(p, int) else IN_SPLITS[IN_NAMES.index(p)] for p in parts)


def _relayout_kernel(w_ref, *out_refs):
    for parts, o_ref in zip(_W_BLOCKS, out_refs):
        col = 0
        for part in parts:
            if isinstance(part, int):
                o_ref[:, col:col + part] = jnp.zeros((o_ref.shape[0], part), o_ref.dtype)
                col += part
            else:
                a, b = _in_cols(part)
                o_ref[:, col:col + b - a] = w_ref[:, a:b].astype(o_ref.dtype)
                col += b - a


def _relayout_w_in(w_in):
    depth, d, d_in = w_in.shape
    tr = 256
    widths = [_block_width(parts) for parts in _W_BLOCKS]
    return pl.pallas_call(
        _relayout_kernel,
        grid=(depth, d // tr),
        in_specs=[pl.BlockSpec((None, tr, d_in), lambda l, r: (l, r, 0))],
        out_specs=[pl.BlockSpec((None, tr, w), lambda l, r: (l, r, 0)) for w in widths],
        out_shape=[jax.ShapeDtypeStruct((depth, d, w), MXU_DTYPE) for w in widths],
        compiler_params=_params(2),
        name="relayout_w_in",
    )(w_in)


_SEG_G = ((0, D_MIX, None, None),)
_SEG_B = ((0, B_WIDTH, PAT_DSA, HEAD_DIM ** -0.5 * LOG2E), (B_WIDTH, B_WIDTH, PAT_DSA, None),
          (2 * B_WIDTH, B_WIDTH, None, None), (3 * B_WIDTH, IDX_HEADS * IDX_DIM, PAT_SMALL, None),
          (3 * B_WIDTH + IDX_HEADS * IDX_DIM, LANES, PAT_SMALL, None),
          (3 * B_WIDTH + IDX_HEADS * IDX_DIM + LANES, LANES, None, None))
_SEG_C = ((0, C_WIDTH, PAT_SMALL, C_QK ** -0.5 * LOG2E), (C_WIDTH, C_WIDTH, PAT_SMALL, None),
          (2 * C_WIDTH, C_WIDTH, None, None))


def kernel(x, p, positions, w_in, w_uq, w_ukv, w_o, norm_g, q_norm_g, kv_norm_g,
           lam_q1, lam_k1, lam_q2, lam_k2, subln_g, w_ple, w_pg, final_g):
    b, s, d = x.shape
    n = b * s
    f32 = jnp.float32
    n_sel = min(TOPK_MAX, s // 4)

    w_a, w_g, w_b, w_c = _relayout_w_in(w_in)
    uq = w_uq.reshape(DEPTH, A_Q_RANK, A_HEADS, A_NOPE + A_ROPE)
    uq = jnp.pad(uq, ((0, 0), (0, 0), (0, 0), (0, A_QK_PAD - A_NOPE - A_ROPE)))
    uq = uq.reshape(DEPTH, A_Q_RANK, A_HEADS * A_QK_PAD).astype(MXU_DTYPE)
    ukv = w_ukv.reshape(DEPTH, A_KV_RANK, A_HEADS, A_NOPE + A_V)
    ukv = jnp.concatenate([ukv[..., :A_NOPE].reshape(DEPTH, A_KV_RANK, A_WIDTH),
                           ukv[..., A_NOPE:].reshape(DEPTH, A_KV_RANK, A_WIDTH)], axis=-1).astype(MXU_DTYPE)
    w_o_c = w_o.astype(MXU_DTYPE)
    w_pg_c = w_pg.astype(MXU_DTYPE)
    w_ple_c = w_ple.astype(MXU_DTYPE)
    pad_lanes = lambda v: jnp.pad(v, (0, LANES - v.shape[0]))

    tabs = _rope_tables(positions.astype(f32).reshape(n, 1))
    h = x.reshape(n, d)
    u = _norm(h, norm_g[0], MXU_DTYPE)
    out = None
    for i in range(DEPTH):
        q_a, k_a, v_a = _mla_proj(u, w_a[i], q_norm_g[i].reshape(1, -1), kv_norm_g[i].reshape(1, -1),
                                  uq[i], ukv[i], tabs)
        (g,) = _proj(u, w_g[i], None, _SEG_G, (f32,), "proj_gates")
        q_b, k_b, v_b, q_i, k_i, w_i = _proj(u, w_b[i], tabs, _SEG_B,
                                             (MXU_DTYPE,) * 5 + (f32,), "proj_dsa")
        q_c, k_c, v_c = _proj(u, w_c[i], tabs, _SEG_C, (MXU_DTYPE,) * 3, "proj_diff")

        r3 = lambda a: a.reshape(b, s, a.shape[-1])
        o_a = _mla_attn(r3(q_a), r3(k_a), r3(v_a))
        o_b = _dsa_attn(r3(q_b), r3(q_i), r3(w_i), r3(k_i), r3(k_b), r3(v_b), n_sel)
        lam_init = 0.8 - 0.6 * math.exp(-0.3 * i)
        consts = jnp.stack([pad_lanes(lam_q1[i]), pad_lanes(lam_k1[i]), pad_lanes(lam_q2[i]),
                            pad_lanes(lam_k2[i]), jnp.full((LANES,), lam_init, f32), subln_g[i],
                            jnp.zeros((LANES,), f32), jnp.zeros((LANES,), f32)]).astype(f32)
        o_c = _diff_attn(r3(q_c), r3(k_c), r3(v_c), consts)

        last = i == DEPTH - 1
        gn = (final_g if last else norm_g[i + 1]).reshape(1, d)
        res = _out_proj(o_a.reshape(n, -1), o_b.reshape(n, -1), o_c.reshape(n, -1), g, h,
                        p[i].reshape(n, PLE_DIM), w_o_c[i], w_pg_c[i], w_ple_c[i], gn, emit_h=not last)
        if last:
            out = res[0]
        else:
            h, u = res
    return out.reshape(b, s, d)
```

```python
import functools
import math

import jax
import jax.numpy as jnp
import numpy as np
from jax import lax
from jax.experimental import pallas as pl
from jax.experimental.pallas import tpu as pltpu

D_MODEL = 2048
DEPTH = 4
CHUNK = 64
ROPE_THETA = 500000.0
NORM_EPS = 1e-6
NEG_INF = -1e30
PLE_DIM = 256
HEAD_DIM = 128

A_HEADS = 6
A_Q_RANK = 384
A_KV_RANK = 256
A_NOPE = 128
A_ROPE = 64
A_V = 128
A_WIDTH = A_HEADS * A_V
B_HEADS = 5
B_WIDTH = B_HEADS * HEAD_DIM
IDX_HEADS = 8
IDX_DIM = 64
TOPK_MAX = 256
C_HEADS = 5
C_QK = 64
C_V = 128
C_WIDTH = C_HEADS * C_V
D_MIX = A_WIDTH + B_WIDTH + C_WIDTH

IN_NAMES = ("c_q", "c_kv", "k_rope", "g_a", "q_b", "k_b", "v_b", "q_idx", "k_idx", "w_idx", "g_b",
            "q_c", "k_c", "v_c", "g_c")
IN_SPLITS = (A_Q_RANK, A_KV_RANK, A_ROPE, A_WIDTH,
             B_WIDTH, B_WIDTH, B_WIDTH, IDX_HEADS * IDX_DIM, IDX_DIM, IDX_HEADS, B_WIDTH,
             C_HEADS * 2 * C_QK, C_HEADS * 2 * C_QK, C_WIDTH, C_WIDTH)

LANES = 128
A_QK_PAD = 256
INT_MIN = -2 ** 31
LOG2E = math.log2(math.e)
MXU_DTYPE = jnp.bfloat16
VMEM_LIMIT = 56 * 1024 * 1024

ROPE_PATTERNS = ((A_ROPE, 64), (HEAD_DIM // 4, 128), (IDX_DIM // 4, 64))
PAT_MLA, PAT_DSA, PAT_SMALL = 0, 1, 2

_NT = (((1,), (1,)), ((), ()))


def _params(n_axes):
    return pltpu.CompilerParams(dimension_semantics=("arbitrary",) * n_axes,
                                vmem_limit_bytes=VMEM_LIMIT)


def _dot(a, b):
    return jnp.dot(a, b, preferred_element_type=jnp.float32)


def _dot_nt(a, b):
    return lax.dot_general(a, b, _NT, preferred_element_type=jnp.float32)


def _sigmoid(x):
    return 1.0 / (1.0 + jnp.exp(-x))


def _rms(x, g):
    return x * lax.rsqrt(jnp.mean(x * x, axis=-1, keepdims=True) + NORM_EPS) * g


def _rope_pattern_rows():
    freq = []
    m_up = np.zeros((3, LANES), np.float32)
    m_dn = np.zeros((3, LANES), np.float32)
    for p, (n_rot, period) in enumerate(ROPE_PATTERNS):
        half = n_rot // 2
        inv = 1.0 / (ROPE_THETA ** (jnp.arange(half, dtype=jnp.float32) * (2.0 / n_rot)))
        one = jnp.concatenate([inv, inv, jnp.zeros((period - n_rot,), jnp.float32)])
        freq.append(jnp.tile(one, LANES // period))
        for lane in range(LANES):
            m = lane % period
            m_up[p, lane] = float(m < half)
            m_dn[p, lane] = float(half <= m < n_rot)
    return jnp.stack(freq), jnp.asarray(m_up), jnp.asarray(m_dn)


def _rope_tab_kernel(pos_ref, f_ref, mu_ref, md_ref, c_ref, sa_ref, sb_ref):
    pos = pos_ref[...]
    for p in range(3):
        ang = pos * f_ref[p:p + 1, :]
        sin = jnp.sin(ang)
        c_ref[p] = jnp.cos(ang)
        sa_ref[p] = -sin * mu_ref[p:p + 1, :]
        sb_ref[p] = sin * md_ref[p:p + 1, :]


def _rope_tables(pos_f32):
    n = pos_f32.shape[0]
    tm = min(n, 1024)
    freq, m_up, m_dn = _rope_pattern_rows()
    row = pl.BlockSpec((3, LANES), lambda i: (0, 0))
    tab = pl.BlockSpec((3, tm, LANES), lambda i: (0, i, 0))
    shape = jax.ShapeDtypeStruct((3, n, LANES), jnp.float32)
    return pl.pallas_call(
        _rope_tab_kernel,
        grid=(n // tm,),
        in_specs=[pl.BlockSpec((tm, 1), lambda i: (i, 0)), row, row, row],
        out_specs=[tab, tab, tab],
        out_shape=[shape, shape, shape],
        compiler_params=_params(1),
        name="rope_tables",
    )(pos_f32, freq, m_up, m_dn)


def _rope128(x, c, sa, sb, half):
    return x * c + pltpu.roll(x, LANES - half, 1) * sa + pltpu.roll(x, half, 1) * sb


def _norm_kernel(x_ref, g_ref, o_ref):
    o_ref[...] = _rms(x_ref[...], g_ref[...]).astype(o_ref.dtype)


def _norm(x, g, out_dtype):
    n, d = x.shape
    tm = min(n, 512)
    return pl.pallas_call(
        _norm_kernel,
        grid=(n // tm,),
        in_specs=[pl.BlockSpec((tm, d), lambda i: (i, 0)), pl.BlockSpec((1, d), lambda i: (0, 0))],
        out_specs=pl.BlockSpec((tm, d), lambda i: (i, 0)),
        out_shape=jax.ShapeDtypeStruct((n, d), out_dtype),
        compiler_params=_params(1),
        name="rms_norm",
    )(x, g.reshape(1, d))


def _proj_kernel(*refs, segs, with_tables):
    u_ref, w_ref = refs[0], refs[1]
    if with_tables:
        c_ref, sa_ref, sb_ref = refs[2:5]
        outs = refs[5:]
    else:
        outs = refs[2:]
    u = u_ref[...]
    for (c0, width, pat, scale), o_ref in zip(segs, outs):
        z = _dot_nt(u, w_ref[c0:c0 + width, :])
        if pat is None and scale is None:
            o_ref[...] = z.astype(o_ref.dtype)
            continue
        for ch in range(width // LANES):
            sl = slice(ch * LANES, (ch + 1) * LANES)
            x = z[:, sl]
            if pat is not None:
                x = _rope128(x, c_ref[pat], sa_ref[pat], sb_ref[pat], ROPE_PATTERNS[pat][0] // 2)
            if scale is not None:
                x = x * scale
            o_ref[:, sl] = x.astype(o_ref.dtype)


def _layer_spec(stacked, layer, **kwargs):
    zeros = (0,) * (stacked.ndim - 1)
    return pl.BlockSpec((None,) + stacked.shape[1:], lambda i: (layer,) + zeros, **kwargs)


def _proj(u, w, layer, tabs, segs, out_dtypes, name):
    n, d = u.shape
    tm = min(n, 512)
    in_specs = [pl.BlockSpec((tm, d), lambda i: (i, 0)), _layer_spec(w, layer)]
    args = [u, w]
    if tabs is not None:
        in_specs += [pl.BlockSpec((3, tm, LANES), lambda i: (0, i, 0))] * 3
        args += list(tabs)
    out_specs = [pl.BlockSpec((tm, s[1]), lambda i: (i, 0)) for s in segs]
    out_shape = [jax.ShapeDtypeStruct((n, s[1]), dt) for s, dt in zip(segs, out_dtypes)]
    return pl.pallas_call(
        functools.partial(_proj_kernel, segs=tuple(segs), with_tables=tabs is not None),
        grid=(n // tm,),
        in_specs=in_specs,
        out_specs=out_specs,
        out_shape=out_shape,
        compiler_params=_params(1),
        name=name,
    )(*args)


def _mla_proj_kernel(u_ref, w_ref, qg_ref, kvg_ref, wuq_ref, wukv_ref, c_ref, sa_ref, sb_ref,
                     q_ref, k_ref, v_ref):
    z = _dot_nt(u_ref[...], w_ref[...])
    c, sa, sb = c_ref[PAT_MLA], sa_ref[PAT_MLA], sb_ref[PAT_MLA]
    half = A_ROPE // 2
    scale = (A_NOPE + A_ROPE) ** -0.5 * LOG2E
    c_q = z[:, :A_Q_RANK]
    c_kv = z[:, A_Q_RANK:A_Q_RANK + A_KV_RANK]
    k_rope = z[:, A_Q_RANK + A_KV_RANK:]
    q = _dot(_rms(c_q, qg_ref[...]).astype(MXU_DTYPE), wuq_ref[...])
    kv = _dot(_rms(c_kv, kvg_ref[...]).astype(MXU_DTYPE), wukv_ref[...])
    k_pe = _rope128(k_rope, c, sa, sb, half).astype(k_ref.dtype)
    for h in range(A_HEADS):
        o = h * A_QK_PAD
        q_ref[:, o:o + A_NOPE] = (q[:, o:o + A_NOPE] * scale).astype(q_ref.dtype)
        q_rot = _rope128(q[:, o + A_NOPE:o + A_QK_PAD], c, sa, sb, half)
        q_ref[:, o + A_NOPE:o + A_QK_PAD] = (q_rot * scale).astype(q_ref.dtype)
        k_ref[:, o:o + A_NOPE] = kv[:, h * A_NOPE:(h + 1) * A_NOPE].astype(k_ref.dtype)
        k_ref[:, o + A_NOPE:o + A_QK_PAD] = k_pe
    v_ref[...] = kv[:, A_WIDTH:].astype(v_ref.dtype)


def _mla_proj(u, w, layer, qg, kvg, wuq, wukv, tabs):
    n, d = u.shape
    tm = min(n, 512)
    full = lambda a: pl.BlockSpec(a.shape, lambda i: (0,) * a.ndim)
    tab = pl.BlockSpec((3, tm, LANES), lambda i: (0, i, 0))
    qk = A_HEADS * A_QK_PAD
    return pl.pallas_call(
        _mla_proj_kernel,
        grid=(n // tm,),
        in_specs=[pl.BlockSpec((tm, d), lambda i: (i, 0)), _layer_spec(w, layer), full(qg), full(kvg),
                  _layer_spec(wuq, layer), _layer_spec(wukv, layer), tab, tab, tab],
        out_specs=[pl.BlockSpec((tm, qk), lambda i: (i, 0)), pl.BlockSpec((tm, qk), lambda i: (i, 0)),
                   pl.BlockSpec((tm, A_WIDTH), lambda i: (i, 0))],
        out_shape=[jax.ShapeDtypeStruct((n, qk), MXU_DTYPE), jax.ShapeDtypeStruct((n, qk), MXU_DTYPE),
                   jax.ShapeDtypeStruct((n, A_WIDTH), MXU_DTYPE)],
        compiler_params=_params(1),
        name="mla_proj",
    )(u, w, qg, kvg, wuq, wukv, *tabs)


def _diag_mask(tq, tk, q0, k0):
    qc = (q0 + lax.broadcasted_iota(jnp.int32, (tq, tk), 0)) // CHUNK
    kc = (k0 + lax.broadcasted_iota(jnp.int32, (tq, tk), 1)) // CHUNK
    return kc <= qc


def _softmax_init(m_ref, l_ref, acc_ref):
    m_ref[...] = jnp.full(m_ref.shape, NEG_INF, jnp.float32)
    l_ref[...] = jnp.zeros(l_ref.shape, jnp.float32)
    acc_ref[...] = jnp.zeros(acc_ref.shape, jnp.float32)


def _online_softmax_step(s, v, m_ref, l_ref, acc_ref):
    chunks = [s[:, c * LANES:(c + 1) * LANES] for c in range(s.shape[1] // LANES)]
    cmax = functools.reduce(jnp.maximum, chunks)
    m_prev = m_ref[...]
    m_new = jnp.maximum(m_prev, jnp.max(cmax, axis=-1, keepdims=True))
    alpha = jnp.exp2(m_prev - m_new)
    ps = [jnp.exp2(c - m_new) for c in chunks]
    l_ref[...] = alpha * l_ref[...] + functools.reduce(jnp.add, ps)
    p = jnp.concatenate([x.astype(v.dtype) for x in ps], axis=1)
    acc_ref[...] = alpha * acc_ref[...] + _dot(p, v)
    m_ref[...] = m_new


def _softmax_finish(l_ref, acc_ref):
    return acc_ref[...] / jnp.sum(l_ref[...], axis=-1, keepdims=True)


MLA_HEADS_PER_STEP = 6


def _mla_attn_kernel(q_ref, k_ref, v_ref, o_ref, m_sc, l_sc, acc_sc, *, tq):
    i = pl.program_id(2)
    _softmax_init(m_sc, l_sc, acc_sc)

    def tile(j, masked):
        rows = pl.ds(pl.multiple_of(j * tq, tq), tq)
        for h in range(MLA_HEADS_PER_STEP):
            qk = slice(h * A_QK_PAD, (h + 1) * A_QK_PAD)
            s = _dot_nt(q_ref[:, qk], k_ref[rows, qk])
            if masked:
                s = jnp.where(_diag_mask(tq, tq, 0, 0), s, NEG_INF)
            _online_softmax_step(s, v_ref[rows, h * A_V:(h + 1) * A_V], m_sc.at[h], l_sc.at[h], acc_sc.at[h])

    def body(j, carry):
        tile(j, False)
        return carry

    lax.fori_loop(0, i, body, 0)
    tile(i, True)
    for h in range(MLA_HEADS_PER_STEP):
        o_ref[:, h * A_V:(h + 1) * A_V] = _softmax_finish(l_sc.at[h], acc_sc.at[h])


def _mla_attn(q, k, v):
    b, s, _ = q.shape
    tq = min(s, 512)
    hp = MLA_HEADS_PER_STEP
    stat = pltpu.VMEM((hp, tq, LANES), jnp.float32)
    return pl.pallas_call(
        functools.partial(_mla_attn_kernel, tq=tq),
        grid=(b, A_HEADS // hp, s // tq),
        in_specs=[pl.BlockSpec((None, tq, hp * A_QK_PAD), lambda bi, h, i: (bi, i, h)),
                  pl.BlockSpec((None, s, hp * A_QK_PAD), lambda bi, h, i: (bi, 0, h),
                               pipeline_mode=pl.Buffered(1)),
                  pl.BlockSpec((None, s, hp * A_V), lambda bi, h, i: (bi, 0, h),
                               pipeline_mode=pl.Buffered(1))],
        out_specs=pl.BlockSpec((None, tq, hp * A_V), lambda bi, h, i: (bi, i, h)),
        out_shape=jax.ShapeDtypeStruct((b, s, A_WIDTH), jnp.float32),
        scratch_shapes=[stat, stat, pltpu.VMEM((hp, tq, A_V), jnp.float32)],
        compiler_params=_params(3),
        name="mla_attn",
    )(q, k, v)


def _diff_attn_kernel(q_ref, k_ref, v_ref, c_ref, o_ref, qs_sc, m_sc, l_sc, acc_sc, *, tq):
    i = pl.program_id(1)
    lane = lax.broadcasted_iota(jnp.int32, (tq, 2 * C_QK), 1)
    for h in range(C_HEADS):
        q = q_ref[:, h * 2 * C_QK:(h + 1) * 2 * C_QK]
        qs_sc[2 * h] = jnp.where(lane < C_QK, q, jnp.zeros_like(q))
        qs_sc[2 * h + 1] = jnp.where(lane >= C_QK, q, jnp.zeros_like(q))
    _softmax_init(m_sc, l_sc, acc_sc)

    def tile(j, masked):
        rows = pl.ds(pl.multiple_of(j * tq, tq), tq)
        for h in range(C_HEADS):
            k = k_ref[rows, h * 2 * C_QK:(h + 1) * 2 * C_QK]
            v = v_ref[rows, h * C_V:(h + 1) * C_V]
            for t in range(2):
                n = 2 * h + t
                s = _dot_nt(qs_sc[n], k)
                if masked:
                    s = jnp.where(_diag_mask(tq, tq, 0, 0), s, NEG_INF)
                _online_softmax_step(s, v, m_sc.at[n], l_sc.at[n], acc_sc.at[n])

    def body(j, carry):
        tile(j, False)
        return carry

    lax.fori_loop(0, i, body, 0)
    tile(i, True)

    c = c_ref[...]
    lam_init = c[4:5, 0:1]
    lam = (jnp.exp(jnp.sum(c[0:1] * c[1:2], axis=-1, keepdims=True))
           - jnp.exp(jnp.sum(c[2:3] * c[3:4], axis=-1, keepdims=True)) + lam_init)
    for h in range(C_HEADS):
        o = (_softmax_finish(l_sc.at[2 * h], acc_sc.at[2 * h])
             - lam * _softmax_finish(l_sc.at[2 * h + 1], acc_sc.at[2 * h + 1]))
        o_ref[:, h * C_V:(h + 1) * C_V] = _rms(o, c[5:6]) * (1.0 - lam_init)


def _diff_attn(q, k, v, consts):
    b, s, _ = q.shape
    tq = min(s, 512)
    n_chain = 2 * C_HEADS
    stat = pltpu.VMEM((n_chain, tq, LANES), jnp.float32)
    resident = lambda width: pl.BlockSpec((None, s, width), lambda bi, i: (bi, 0, 0),
                                          pipeline_mode=pl.Buffered(1))
    return pl.pallas_call(
        functools.partial(_diff_attn_kernel, tq=tq),
        grid=(b, s // tq),
        in_specs=[pl.BlockSpec((None, tq, C_WIDTH), lambda bi, i: (bi, i, 0)),
                  resident(C_WIDTH), resident(C_WIDTH),
                  pl.BlockSpec((8, LANES), lambda bi, i: (0, 0))],
        out_specs=pl.BlockSpec((None, tq, C_WIDTH), lambda bi, i: (bi, i, 0)),
        out_shape=jax.ShapeDtypeStruct((b, s, C_WIDTH), jnp.float32),
        scratch_shapes=[pltpu.VMEM((n_chain, tq, 2 * C_QK), MXU_DTYPE), stat, stat,
                        pltpu.VMEM((n_chain, tq, C_V), jnp.float32)],
        compiler_params=_params(2),
        name="diff_attn",
    )(q, k, v, consts)


SUBLANES = 8
COUNT_CHAINS = 4
PACKED_ROWS = 16
HALF_MIN = -2 ** 15
IDX_GROUP = 4


def _tree_sum(parts):
    while len(parts) > 1:
        parts = [a + b for a, b in zip(parts[0::2], parts[1::2])] + parts[len(parts) & ~1:]
    return parts[0]


def _dsa_kernel(qb_ref, qi_ref, w_ref, ki_ref, kb_ref, vb_ref, o_ref,
                keys_sc, half_sc, qz_sc, m_sc, l_sc, acc_sc, *, tq, n_sel):
    i = pl.program_id(1)
    tk = tq
    n_tiles = i + 1
    f32 = jnp.float32

    lane = lax.broadcasted_iota(jnp.int32, (tq, LANES), 1)
    for h in range(IDX_HEADS):
        pair = qi_ref[:, (h // 2) * LANES:(h // 2 + 1) * LANES]
        keep = (lane < IDX_DIM) if h % 2 == 0 else (lane >= IDX_DIM)
        qz_sc[h * tq:(h + 1) * tq, :] = jnp.where(keep, pair, jnp.zeros_like(pair))
    w_t = w_ref[...].T

    def score_tile(j, masked):
        rows = pl.ds(pl.multiple_of(j * tk, tk), tk)
        ki = ki_ref[rows, :]
        sc = jnp.zeros((tk, tq), f32)
        for h0 in range(0, IDX_HEADS, IDX_GROUP):
            rel = _dot_nt(ki, qz_sc[h0 * tq:(h0 + IDX_GROUP) * tq, :])
            for g in range(IDX_GROUP):
                sc = sc + jnp.maximum(rel[:, g * tq:(g + 1) * tq], 0.0) * w_t[h0 + g:h0 + g + 1, :]
        if masked:
            kc = (j * tk + lax.broadcasted_iota(jnp.int32, (tk, tq), 0)) // CHUNK
            qc = (i * tq + lax.broadcasted_iota(jnp.int32, (tk, tq), 1)) // CHUNK
            sc = jnp.where(kc <= qc, sc, NEG_INF)
        bits = lax.bitcast_convert_type(sc, jnp.int32)
        key = jnp.where(bits < 0, INT_MIN - bits, bits)
        key = jnp.where(sc > 0.5 * NEG_INF, key, INT_MIN)
        keys_sc[j] = key
        half_sc[j] = (key >> 16).astype(jnp.int16)

    def score_body(j, carry):
        score_tile(j, False)
        return carry

    lax.fori_loop(0, n_tiles - 1, score_body, 0)
    score_tile(n_tiles - 1, True)

    def count_ge(t):
        def body(j, accs):
            accs = list(accs)
            for n, r in enumerate(range(0, tk, SUBLANES)):
                hit = jnp.where(keys_sc[j, r:r + SUBLANES, :] >= t, 1.0, 0.0)
                accs[n % COUNT_CHAINS] = accs[n % COUNT_CHAINS] + hit
            return tuple(accs)
        zero = jnp.zeros((SUBLANES, tq), f32)
        accs = lax.fori_loop(0, n_tiles, body, (zero,) * COUNT_CHAINS)
        return jnp.sum(_tree_sum(list(accs)), axis=0, keepdims=True)

    def count16(ref, t, strict):
        t16 = t.astype(jnp.int16)
        one, zero = jnp.ones((), jnp.int16), jnp.zeros((), jnp.int16)

        def body(j, accs):
            accs = list(accs)
            for n, r in enumerate(range(0, tk, PACKED_ROWS)):
                x = ref[j, r:r + PACKED_ROWS, :]
                hit = jnp.where((x > t16) if strict else (x >= t16), one, zero)
                accs[n % COUNT_CHAINS] = accs[n % COUNT_CHAINS] + hit
            return tuple(accs)

        accs = lax.fori_loop(0, n_tiles, body, (jnp.zeros((PACKED_ROWS, tq), jnp.int16),) * COUNT_CHAINS)
        return jnp.sum(_tree_sum(list(accs)).astype(jnp.int32), axis=0, keepdims=True)

    def kth_largest16(ref, k_need):
        def step(it, t):
            cand = t + lax.shift_left(jnp.int32(1), jnp.int32(15) - it)
            return jnp.where(count16(ref, cand, False) >= k_need, cand, t)
        return lax.fori_loop(0, 16, step, jnp.full((1, tq), HALF_MIN, jnp.int32))

    thr_hi = kth_largest16(half_sc, n_sel)
    k_lo = n_sel - count16(half_sc, thr_hi, True)
    thr_hi16 = thr_hi.astype(jnp.int16)

    def lo_body(j, carry):
        lo = ((keys_sc[j] & 0xFFFF) + HALF_MIN).astype(jnp.int16)
        half_sc[j] = jnp.where(half_sc[j] == thr_hi16, lo, jnp.int16(HALF_MIN))
        return carry

    lax.fori_loop(0, n_tiles, lo_body, 0)
    thr_lo = kth_largest16(half_sc, k_lo)
    thr = thr_hi * 65536 + (thr_lo - HALF_MIN)
    thr = jnp.maximum(thr, INT_MIN + 1)
    k_f = float(n_sel)
    cnt_ge = count_ge(thr)

    @pl.when(jnp.max(cnt_ge) > k_f)
    def _():
        keep_n = k_f - count_ge(thr + 1)
        tri = (lax.broadcasted_iota(jnp.int32, (tk, tk), 0)
               >= lax.broadcasted_iota(jnp.int32, (tk, tk), 1)).astype(MXU_DTYPE)

        def body(j, carry):
            kt = keys_sc[j]
            eq = kt == thr
            rank = carry + _dot(tri, jnp.where(eq, 1.0, 0.0).astype(MXU_DTYPE))
            keys_sc[j] = jnp.where(eq & (rank > keep_n), INT_MIN, kt)
            return rank[tk - 1:tk, :]

        lax.fori_loop(0, n_tiles, body, jnp.zeros((1, tq), f32))

    def bias_body(j, carry):
        bias = jnp.where(keys_sc[j] >= thr, 0.0, NEG_INF).T
        keys_sc[j] = lax.bitcast_convert_type(bias, jnp.int32)
        return carry

    lax.fori_loop(0, n_tiles, bias_body, 0)
    _softmax_init(m_sc, l_sc, acc_sc)

    def attn_body(j, carry):
        rows = pl.ds(pl.multiple_of(j * tk, tk), tk)
        bias = lax.bitcast_convert_type(keys_sc[j], f32)
        for h in range(B_HEADS):
            sl = slice(h * HEAD_DIM, (h + 1) * HEAD_DIM)
            s = _dot_nt(qb_ref[:, sl], kb_ref[rows, sl]) + bias
            _online_softmax_step(s, vb_ref[rows, sl], m_sc.at[h], l_sc.at[h], acc_sc.at[h])
        return carry

    lax.fori_loop(0, n_tiles, attn_body, 0)
    for h in range(B_HEADS):
        o_ref[:, h * HEAD_DIM:(h + 1) * HEAD_DIM] = _softmax_finish(l_sc.at[h], acc_sc.at[h])


def _dsa_attn(qb, qi, w, ki, kb, vb, n_sel):
    b, s, _ = qb.shape
    tq = min(s, 512)
    blk = lambda width: pl.BlockSpec((None, tq, width), lambda bi, i: (bi, i, 0))
    res = lambda width: pl.BlockSpec((None, s, width), lambda bi, i: (bi, 0, 0), pipeline_mode=pl.Buffered(1))
    return pl.pallas_call(
        functools.partial(_dsa_kernel, tq=tq, n_sel=n_sel),
        grid=(b, s // tq),
        in_specs=[blk(B_WIDTH), blk(IDX_HEADS * IDX_DIM), blk(LANES), res(LANES), res(B_WIDTH), res(B_WIDTH)],
        out_specs=blk(B_WIDTH),
        out_shape=jax.ShapeDtypeStruct((b, s, B_WIDTH), jnp.float32),
        scratch_shapes=[pltpu.VMEM((s // tq, tq, tq), jnp.int32),
                        pltpu.VMEM((s // tq, tq, tq), jnp.int16),
                        pltpu.VMEM((IDX_HEADS * tq, LANES), MXU_DTYPE),
                        pltpu.VMEM((B_HEADS, tq, LANES), jnp.float32),
                        pltpu.VMEM((B_HEADS, tq, LANES), jnp.float32),
                        pltpu.VMEM((B_HEADS, tq, HEAD_DIM), jnp.float32)],
        compiler_params=_params(2),
        name="dsa_attn",
    )(qb, qi, w, ki, kb, vb)


def _out_kernel(oa_ref, ob_ref, oc_ref, g_ref, h_ref, p_ref, wo_ref, wpg_ref, wple_ref, gn_ref,
                *out_refs, emit_h):
    g = g_ref[...]
    sg = g * _sigmoid(g)
    h1 = h_ref[...]
    col = 0
    for o_ref in (oa_ref, ob_ref, oc_ref):
        width = o_ref.shape[1]
        mixed = (o_ref[...] * sg[:, col:col + width]).astype(MXU_DTYPE)
        h1 = h1 + _dot(mixed, wo_ref[col:col + width, :])
        col += width
    gate = _sigmoid(_dot(h1.astype(MXU_DTYPE), wpg_ref[...]))
    h2 = h1 + _dot(p_ref[...].astype(MXU_DTYPE), wple_ref[...]) * gate
    if emit_h:
        out_refs[0][...] = h2
    out_refs[-1][...] = _rms(h2, gn_ref[...]).astype(out_refs[-1].dtype)


def _out_proj(oa, ob, oc, g, h, p, wo, wpg, wple, layer, gn, emit_h):
    n, d = h.shape
    tm = min(n, 256)
    row = lambda a: pl.BlockSpec((tm, a.shape[1]), lambda i: (i, 0))
    const = lambda a: _layer_spec(a, layer, pipeline_mode=pl.Buffered(1))
    out_specs = [pl.BlockSpec((tm, d), lambda i: (i, 0))]
    out_shape = [jax.ShapeDtypeStruct((n, d), MXU_DTYPE if emit_h else jnp.float32)]
    if emit_h:
        out_specs = [pl.BlockSpec((tm, d), lambda i: (i, 0))] + out_specs
        out_shape = [jax.ShapeDtypeStruct((n, d), jnp.float32)] + out_shape
    return pl.pallas_call(
        functools.partial(_out_kernel, emit_h=emit_h),
        grid=(n // tm,),
        in_specs=[row(oa), row(ob), row(oc), row(g), row(h),
                  pl.BlockSpec((None, tm, p.shape[2]), lambda i: (layer, i, 0)),
                  const(wo), const(wpg), const(wple), pl.BlockSpec((1, d), lambda i: (0, 0))],
        out_specs=out_specs,
        out_shape=out_shape,
        compiler_params=_params(1),
        name="out_proj",
    )(oa, ob, oc, g, h, p, wo, wpg, wple, gn)


def _in_cols(name):
    k = IN_NAMES.index(name)
    start = sum(IN_SPLITS[:k])
    return start, start + IN_SPLITS[k]


_W_BLOCKS = (("c_q", "c_kv", "k_rope", LANES - A_ROPE),
             ("g_a", "g_b", "g_c"),
             ("q_b", "k_b", "v_b", "q_idx", "k_idx", "k_idx", "w_idx", LANES - IDX_HEADS),
             ("q_c", "k_c", "v_c"))


def _block_width(parts):
    return sum(p if isinstance(p, int) else IN_SPLITS[IN_NAMES.index(p)] for p in parts)


def _relayout_kernel(wt_ref, *out_refs):
    for parts, o_ref in zip(_W_BLOCKS, out_refs):
        row, pieces = 0, []
        for part in parts:
            if isinstance(part, int):
                pieces.append(jnp.zeros((part, wt_ref.shape[1]), jnp.float32))
            else:
                a, b = _in_cols(part)
                pieces.append(wt_ref[a:b, :])
            rows = sum(x.shape[0] for x in pieces)
            if rows % PACKED_ROWS == 0:
                o_ref[row:row + rows, :] = jnp.concatenate(pieces, axis=0).astype(o_ref.dtype)
                row, pieces = row + rows, []
        assert not pieces


def _relayout_w_in(w_in):
    depth, d, d_in = w_in.shape
    wt = jnp.transpose(w_in, (0, 2, 1))
    tc = 256
    widths = [_block_width(parts) for parts in _W_BLOCKS]
    return pl.pallas_call(
        _relayout_kernel,
        grid=(depth, d // tc),
        in_specs=[pl.BlockSpec((None, d_in, tc), lambda l, c: (l, 0, c))],
        out_specs=[pl.BlockSpec((None, w, tc), lambda l, c: (l, 0, c)) for w in widths],
        out_shape=[jax.ShapeDtypeStruct((depth, w, d), MXU_DTYPE) for w in widths],
        compiler_params=_params(2),
        name="relayout_w_in",
    )(wt)


_SEG_G = ((0, D_MIX, None, None),)
_SEG_B = ((0, B_WIDTH, PAT_DSA, HEAD_DIM ** -0.5 * LOG2E), (B_WIDTH, B_WIDTH, PAT_DSA, None),
          (2 * B_WIDTH, B_WIDTH, None, None), (3 * B_WIDTH, IDX_HEADS * IDX_DIM, PAT_SMALL, None),
          (3 * B_WIDTH + IDX_HEADS * IDX_DIM, LANES, PAT_SMALL, None),
          (3 * B_WIDTH + IDX_HEADS * IDX_DIM + LANES, LANES, None, None))
_SEG_C = ((0, C_WIDTH, PAT_SMALL, C_QK ** -0.5 * LOG2E), (C_WIDTH, C_WIDTH, PAT_SMALL, None),
          (2 * C_WIDTH, C_WIDTH, None, None))


def kernel(x, p, positions, w_in, w_uq, w_ukv, w_o, norm_g, q_norm_g, kv_norm_g,
           lam_q1, lam_k1, lam_q2, lam_k2, subln_g, w_ple, w_pg, final_g):
    b, s, d = x.shape
    n = b * s
    f32 = jnp.float32
    n_sel = min(TOPK_MAX, s // 4)

    w_a, w_g, w_b, w_c = _relayout_w_in(w_in)
    uq = w_uq.reshape(DEPTH, A_Q_RANK, A_HEADS, A_NOPE + A_ROPE)
    uq = jnp.pad(uq, ((0, 0), (0, 0), (0, 0), (0, A_QK_PAD - A_NOPE - A_ROPE)))
    uq = uq.reshape(DEPTH, A_Q_RANK, A_HEADS * A_QK_PAD).astype(MXU_DTYPE)
    ukv = w_ukv.reshape(DEPTH, A_KV_RANK, A_HEADS, A_NOPE + A_V)
    ukv = jnp.concatenate([ukv[..., :A_NOPE].reshape(DEPTH, A_KV_RANK, A_WIDTH),
                           ukv[..., A_NOPE:].reshape(DEPTH, A_KV_RANK, A_WIDTH)], axis=-1).astype(MXU_DTYPE)
    w_o_c = w_o.astype(MXU_DTYPE)
    w_pg_c = w_pg.astype(MXU_DTYPE)
    w_ple_c = w_ple.astype(MXU_DTYPE)
    pad_lanes = lambda v: jnp.pad(v, (0, LANES - v.shape[0]))

    tabs = _rope_tables(positions.astype(f32).reshape(n, 1))
    h = x.reshape(n, d)
    u = _norm(h, norm_g[0], MXU_DTYPE)
    out = None
    for i in range(DEPTH):
        q_a, k_a, v_a = _mla_proj(u, w_a, i, q_norm_g[i].reshape(1, -1), kv_norm_g[i].reshape(1, -1),
                                  uq, ukv, tabs)
        (g,) = _proj(u, w_g, i, None, _SEG_G, (f32,), "proj_gates")
        q_b, k_b, v_b, q_i, k_i, w_i = _proj(u, w_b, i, tabs, _SEG_B,
                                             (MXU_DTYPE,) * 5 + (f32,), "proj_dsa")
        q_c, k_c, v_c = _proj(u, w_c, i, tabs, _SEG_C, (MXU_DTYPE,) * 3, "proj_diff")

        r3 = lambda a: a.reshape(b, s, a.shape[-1])
        o_a = _mla_attn(r3(q_a), r3(k_a), r3(v_a))
        o_b = _dsa_attn(r3(q_b), r3(q_i), r3(w_i), r3(k_i), r3(k_b), r3(v_b), n_sel)
        lam_init = 0.8 - 0.6 * math.exp(-0.3 * i)
        consts = jnp.stack([pad_lanes(lam_q1[i]), pad_lanes(lam_k1[i]), pad_lanes(lam_q2[i]),
                            pad_lanes(lam_k2[i]), jnp.full((LANES,), lam_init, f32), subln_g[i],
                            jnp.zeros((LANES,), f32), jnp.zeros((LANES,), f32)]).astype(f32)
        o_c = _diff_attn(r3(q_c), r3(k_c), r3(v_c), consts)

        last = i == DEPTH - 1
        gn = (final_g if last else norm_g[i + 1]).reshape(1, d)
        res = _out_proj(o_a.reshape(n, -1), o_b.reshape(n, -1), o_c.reshape(n, -1), g, h,
                        p.reshape(DEPTH, n, PLE_DIM), w_o_c, w_pg_c, w_ple_c, i, gn, emit_h=not last)
        if last:
            out = res[0]
        else:
            h, u = res
    return out.reshape(b, s, d)
```

```python
import functools
import math

import jax
import jax.numpy as jnp
import numpy as np
from jax import lax
from jax.experimental import pallas as pl
from jax.experimental.pallas import tpu as pltpu

D_MODEL = 2048
DEPTH = 4
CHUNK = 64
ROPE_THETA = 500000.0
NORM_EPS = 1e-6
NEG_INF = -1e30
PLE_DIM = 256
HEAD_DIM = 128

A_HEADS = 6
A_Q_RANK = 384
A_KV_RANK = 256
A_NOPE = 128
A_ROPE = 64
A_V = 128
A_WIDTH = A_HEADS * A_V
B_HEADS = 5
B_WIDTH = B_HEADS * HEAD_DIM
IDX_HEADS = 8
IDX_DIM = 64
TOPK_MAX = 256
C_HEADS = 5
C_QK = 64
C_V = 128
C_WIDTH = C_HEADS * C_V
D_MIX = A_WIDTH + B_WIDTH + C_WIDTH

IN_NAMES = ("c_q", "c_kv", "k_rope", "g_a", "q_b", "k_b", "v_b", "q_idx", "k_idx", "w_idx", "g_b",
            "q_c", "k_c", "v_c", "g_c")
IN_SPLITS = (A_Q_RANK, A_KV_RANK, A_ROPE, A_WIDTH,
             B_WIDTH, B_WIDTH, B_WIDTH, IDX_HEADS * IDX_DIM, IDX_DIM, IDX_HEADS, B_WIDTH,
             C_HEADS * 2 * C_QK, C_HEADS * 2 * C_QK, C_WIDTH, C_WIDTH)

LANES = 128
A_QK_PAD = 256
INT_MIN = -2 ** 31
LOG2E = math.log2(math.e)
MXU_DTYPE = jnp.bfloat16
VMEM_LIMIT = 56 * 1024 * 1024

ROPE_PATTERNS = ((A_ROPE, 64), (HEAD_DIM // 4, 128), (IDX_DIM // 4, 64))
PAT_MLA, PAT_DSA, PAT_SMALL = 0, 1, 2

_NT = (((1,), (1,)), ((), ()))


def _params(n_axes):
    return pltpu.CompilerParams(dimension_semantics=("arbitrary",) * n_axes,
                                vmem_limit_bytes=VMEM_LIMIT)


def _dot(a, b):
    return jnp.dot(a, b, preferred_element_type=jnp.float32)


def _dot_nt(a, b):
    return lax.dot_general(a, b, _NT, preferred_element_type=jnp.float32)


def _sigmoid(x):
    return 1.0 / (1.0 + jnp.exp(-x))


def _rms(x, g):
    return x * lax.rsqrt(jnp.mean(x * x, axis=-1, keepdims=True) + NORM_EPS) * g


def _rope_pattern_rows():
    freq = []
    m_up = np.zeros((3, LANES), np.float32)
    m_dn = np.zeros((3, LANES), np.float32)
    for p, (n_rot, period) in enumerate(ROPE_PATTERNS):
        half = n_rot // 2
        inv = 1.0 / (ROPE_THETA ** (jnp.arange(half, dtype=jnp.float32) * (2.0 / n_rot)))
        one = jnp.concatenate([inv, inv, jnp.zeros((period - n_rot,), jnp.float32)])
        freq.append(jnp.tile(one, LANES // period))
        for lane in range(LANES):
            m = lane % period
            m_up[p, lane] = float(m < half)
            m_dn[p, lane] = float(half <= m < n_rot)
    return jnp.stack(freq), jnp.asarray(m_up), jnp.asarray(m_dn)


def _rope_tab_kernel(pos_ref, f_ref, mu_ref, md_ref, c_ref, sa_ref, sb_ref):
    pos = pos_ref[...]
    for p in range(3):
        ang = pos * f_ref[p:p + 1, :]
        sin = jnp.sin(ang)
        c_ref[p] = jnp.cos(ang)
        sa_ref[p] = -sin * mu_ref[p:p + 1, :]
        sb_ref[p] = sin * md_ref[p:p + 1, :]


def _rope_tables(pos_f32):
    n = pos_f32.shape[0]
    tm = min(n, 1024)
    freq, m_up, m_dn = _rope_pattern_rows()
    row = pl.BlockSpec((3, LANES), lambda i: (0, 0))
    tab = pl.BlockSpec((3, tm, LANES), lambda i: (0, i, 0))
    shape = jax.ShapeDtypeStruct((3, n, LANES), jnp.float32)
    return pl.pallas_call(
        _rope_tab_kernel,
        grid=(n // tm,),
        in_specs=[pl.BlockSpec((tm, 1), lambda i: (i, 0)), row, row, row],
        out_specs=[tab, tab, tab],
        out_shape=[shape, shape, shape],
        compiler_params=_params(1),
        name="rope_tables",
    )(pos_f32, freq, m_up, m_dn)


def _rope128(x, c, sa, sb, half):
    return x * c + pltpu.roll(x, LANES - half, 1) * sa + pltpu.roll(x, half, 1) * sb


def _norm_kernel(x_ref, g_ref, o_ref):
    o_ref[...] = _rms(x_ref[...], g_ref[...]).astype(o_ref.dtype)


def _norm(x, g, out_dtype):
    n, d = x.shape
    tm = min(n, 512)
    return pl.pallas_call(
        _norm_kernel,
        grid=(n // tm,),
        in_specs=[pl.BlockSpec((tm, d), lambda i: (i, 0)), pl.BlockSpec((1, d), lambda i: (0, 0))],
        out_specs=pl.BlockSpec((tm, d), lambda i: (i, 0)),
        out_shape=jax.ShapeDtypeStruct((n, d), out_dtype),
        compiler_params=_params(1),
        name="rms_norm",
    )(x, g.reshape(1, d))


def _proj_kernel(*refs, segs, with_tables):
    u_ref, w_ref = refs[0], refs[1]
    if with_tables:
        c_ref, sa_ref, sb_ref = refs[2:5]
        outs = refs[5:]
    else:
        outs = refs[2:]
    u = u_ref[...]
    for (c0, width, pat, scale), o_ref in zip(segs, outs):
        z = _dot_nt(u, w_ref[c0:c0 + width, :])
        if pat is None and scale is None:
            o_ref[...] = z.astype(o_ref.dtype)
            continue
        for ch in range(width // LANES):
            sl = slice(ch * LANES, (ch + 1) * LANES)
            x = z[:, sl]
            if pat is not None:
                x = _rope128(x, c_ref[pat], sa_ref[pat], sb_ref[pat], ROPE_PATTERNS[pat][0] // 2)
            if scale is not None:
                x = x * scale
            o_ref[:, sl] = x.astype(o_ref.dtype)


def _layer_spec(stacked, layer, **kwargs):
    zeros = (0,) * (stacked.ndim - 1)
    return pl.BlockSpec((None,) + stacked.shape[1:], lambda i: (layer,) + zeros, **kwargs)


def _proj(u, w, layer, tabs, segs, out_dtypes, name):
    n, d = u.shape
    tm = min(n, 512)
    in_specs = [pl.BlockSpec((tm, d), lambda i: (i, 0)), _layer_spec(w, layer)]
    args = [u, w]
    if tabs is not None:
        in_specs += [pl.BlockSpec((3, tm, LANES), lambda i: (0, i, 0))] * 3
        args += list(tabs)
    out_specs = [pl.BlockSpec((tm, s[1]), lambda i: (i, 0)) for s in segs]
    out_shape = [jax.ShapeDtypeStruct((n, s[1]), dt) for s, dt in zip(segs, out_dtypes)]
    return pl.pallas_call(
        functools.partial(_proj_kernel, segs=tuple(segs), with_tables=tabs is not None),
        grid=(n // tm,),
        in_specs=in_specs,
        out_specs=out_specs,
        out_shape=out_shape,
        compiler_params=_params(1),
        name=name,
    )(*args)


def _mla_proj_kernel(u_ref, w_ref, qg_ref, kvg_ref, wuq_ref, wukv_ref, c_ref, sa_ref, sb_ref,
                     q_ref, k_ref, v_ref):
    z = _dot_nt(u_ref[...], w_ref[...])
    c, sa, sb = c_ref[PAT_MLA], sa_ref[PAT_MLA], sb_ref[PAT_MLA]
    half = A_ROPE // 2
    scale = (A_NOPE + A_ROPE) ** -0.5 * LOG2E
    c_q = z[:, :A_Q_RANK]
    c_kv = z[:, A_Q_RANK:A_Q_RANK + A_KV_RANK]
    k_rope = z[:, A_Q_RANK + A_KV_RANK:]
    q = _dot(_rms(c_q, qg_ref[...]).astype(MXU_DTYPE), wuq_ref[...])
    kv = _dot(_rms(c_kv, kvg_ref[...]).astype(MXU_DTYPE), wukv_ref[...])
    k_pe = _rope128(k_rope, c, sa, sb, half).astype(k_ref.dtype)
    for h in range(A_HEADS):
        o = h * A_QK_PAD
        q_ref[:, o:o + A_NOPE] = (q[:, o:o + A_NOPE] * scale).astype(q_ref.dtype)
        q_rot = _rope128(q[:, o + A_NOPE:o + A_QK_PAD], c, sa, sb, half)
        q_ref[:, o + A_NOPE:o + A_QK_PAD] = (q_rot * scale).astype(q_ref.dtype)
        k_ref[:, o:o + A_NOPE] = kv[:, h * A_NOPE:(h + 1) * A_NOPE].astype(k_ref.dtype)
        k_ref[:, o + A_NOPE:o + A_QK_PAD] = k_pe
    v_ref[...] = kv[:, A_WIDTH:].astype(v_ref.dtype)


def _mla_proj(u, w, layer, qg, kvg, wuq, wukv, tabs):
    n, d = u.shape
    tm = min(n, 512)
    full = lambda a: pl.BlockSpec(a.shape, lambda i: (0,) * a.ndim)
    tab = pl.BlockSpec((3, tm, LANES), lambda i: (0, i, 0))
    qk = A_HEADS * A_QK_PAD
    return pl.pallas_call(
        _mla_proj_kernel,
        grid=(n // tm,),
        in_specs=[pl.BlockSpec((tm, d), lambda i: (i, 0)), _layer_spec(w, layer), full(qg), full(kvg),
                  _layer_spec(wuq, layer), _layer_spec(wukv, layer), tab, tab, tab],
        out_specs=[pl.BlockSpec((tm, qk), lambda i: (i, 0)), pl.BlockSpec((tm, qk), lambda i: (i, 0)),
                   pl.BlockSpec((tm, A_WIDTH), lambda i: (i, 0))],
        out_shape=[jax.ShapeDtypeStruct((n, qk), MXU_DTYPE), jax.ShapeDtypeStruct((n, qk), MXU_DTYPE),
                   jax.ShapeDtypeStruct((n, A_WIDTH), MXU_DTYPE)],
        compiler_params=_params(1),
        name="mla_proj",
    )(u, w, qg, kvg, wuq, wukv, *tabs)


def _diag_mask(tq, tk, q0, k0):
    qc = (q0 + lax.broadcasted_iota(jnp.int32, (tq, tk), 0)) // CHUNK
    kc = (k0 + lax.broadcasted_iota(jnp.int32, (tq, tk), 1)) // CHUNK
    return kc <= qc


def _softmax_init(m_ref, acc_ref):
    m_ref[...] = jnp.full(m_ref.shape, NEG_INF, jnp.float32)
    acc_ref[...] = jnp.zeros(acc_ref.shape, jnp.float32)


def _online_softmax_step(s, v, m_ref, acc_ref):
    chunks = [s[:, c * LANES:(c + 1) * LANES] for c in range(s.shape[1] // LANES)]
    cmax = functools.reduce(jnp.maximum, chunks)
    m_prev = m_ref[...]
    m_new = jnp.maximum(m_prev, jnp.max(cmax, axis=-1, keepdims=True))
    alpha = jnp.exp2(m_prev - m_new)
    p = jnp.concatenate([jnp.exp2(c - m_new).astype(v.dtype) for c in chunks], axis=1)
    v_ones = jnp.concatenate([v, jnp.ones_like(v)], axis=1)
    acc_ref[...] = jnp.concatenate([alpha, alpha], axis=1) * acc_ref[...] + _dot(p, v_ones)
    m_ref[...] = m_new


def _softmax_finish(acc_ref):
    acc = acc_ref[...]
    return acc[:, :LANES] / acc[:, LANES:]


MLA_HEADS_PER_STEP = 6


def _mla_attn_kernel(q_ref, k_ref, v_ref, o_ref, m_sc, acc_sc, *, tq):
    i = pl.program_id(2)
    _softmax_init(m_sc, acc_sc)

    def tile(j, masked):
        rows = pl.ds(pl.multiple_of(j * tq, tq), tq)
        for h in range(MLA_HEADS_PER_STEP):
            qk = slice(h * A_QK_PAD, (h + 1) * A_QK_PAD)
            s = _dot_nt(q_ref[:, qk], k_ref[rows, qk])
            if masked:
                s = jnp.where(_diag_mask(tq, tq, 0, 0), s, NEG_INF)
            _online_softmax_step(s, v_ref[rows, h * A_V:(h + 1) * A_V], m_sc.at[h], acc_sc.at[h])

    def body(j, carry):
        tile(j, False)
        return carry

    lax.fori_loop(0, i, body, 0)
    tile(i, True)
    for h in range(MLA_HEADS_PER_STEP):
        o_ref[:, h * A_V:(h + 1) * A_V] = _softmax_finish(acc_sc.at[h])


def _mla_attn(q, k, v):
    b, s, _ = q.shape
    tq = min(s, 512)
    hp = MLA_HEADS_PER_STEP
    stat = pltpu.VMEM((hp, tq, LANES), jnp.float32)
    return pl.pallas_call(
        functools.partial(_mla_attn_kernel, tq=tq),
        grid=(b, A_HEADS // hp, s // tq),
        in_specs=[pl.BlockSpec((None, tq, hp * A_QK_PAD), lambda bi, h, i: (bi, i, h)),
                  pl.BlockSpec((None, s, hp * A_QK_PAD), lambda bi, h, i: (bi, 0, h),
                               pipeline_mode=pl.Buffered(1)),
                  pl.BlockSpec((None, s, hp * A_V), lambda bi, h, i: (bi, 0, h),
                               pipeline_mode=pl.Buffered(1))],
        out_specs=pl.BlockSpec((None, tq, hp * A_V), lambda bi, h, i: (bi, i, h)),
        out_shape=jax.ShapeDtypeStruct((b, s, A_WIDTH), jnp.float32),
        scratch_shapes=[stat, pltpu.VMEM((hp, tq, 2 * LANES), jnp.float32)],
        compiler_params=_params(3),
        name="mla_attn",
    )(q, k, v)


def _diff_attn_kernel(q_ref, k_ref, v_ref, c_ref, o_ref, qs_sc, m_sc, acc_sc, *, tq):
    i = pl.program_id(1)
    lane = lax.broadcasted_iota(jnp.int32, (tq, 2 * C_QK), 1)
    for h in range(C_HEADS):
        q = q_ref[:, h * 2 * C_QK:(h + 1) * 2 * C_QK]
        qs_sc[2 * h] = jnp.where(lane < C_QK, q, jnp.zeros_like(q))
        qs_sc[2 * h + 1] = jnp.where(lane >= C_QK, q, jnp.zeros_like(q))
    _softmax_init(m_sc, acc_sc)

    def tile(j, masked):
        rows = pl.ds(pl.multiple_of(j * tq, tq), tq)
        for h in range(C_HEADS):
            k = k_ref[rows, h * 2 * C_QK:(h + 1) * 2 * C_QK]
            v = v_ref[rows, h * C_V:(h + 1) * C_V]
            for t in range(2):
                n = 2 * h + t
                s = _dot_nt(qs_sc[n], k)
                if masked:
                    s = jnp.where(_diag_mask(tq, tq, 0, 0), s, NEG_INF)
                _online_softmax_step(s, v, m_sc.at[n], acc_sc.at[n])

    def body(j, carry):
        tile(j, False)
        return carry

    lax.fori_loop(0, i, body, 0)
    tile(i, True)

    c = c_ref[...]
    lam_init = c[4:5, 0:1]
    lam = (jnp.exp(jnp.sum(c[0:1] * c[1:2], axis=-1, keepdims=True))
           - jnp.exp(jnp.sum(c[2:3] * c[3:4], axis=-1, keepdims=True)) + lam_init)
    for h in range(C_HEADS):
        o = _softmax_finish(acc_sc.at[2 * h]) - lam * _softmax_finish(acc_sc.at[2 * h + 1])
        o_ref[:, h * C_V:(h + 1) * C_V] = _rms(o, c[5:6]) * (1.0 - lam_init)


def _diff_attn(q, k, v, consts):
    b, s, _ = q.shape
    tq = min(s, 512)
    n_chain = 2 * C_HEADS
    stat = pltpu.VMEM((n_chain, tq, LANES), jnp.float32)
    resident = lambda width: pl.BlockSpec((None, s, width), lambda bi, i: (bi, 0, 0),
                                          pipeline_mode=pl.Buffered(1))
    return pl.pallas_call(
        functools.partial(_diff_attn_kernel, tq=tq),
        grid=(b, s // tq),
        in_specs=[pl.BlockSpec((None, tq, C_WIDTH), lambda bi, i: (bi, i, 0)),
                  resident(C_WIDTH), resident(C_WIDTH),
                  pl.BlockSpec((8, LANES), lambda bi, i: (0, 0))],
        out_specs=pl.BlockSpec((None, tq, C_WIDTH), lambda bi, i: (bi, i, 0)),
        out_shape=jax.ShapeDtypeStruct((b, s, C_WIDTH), jnp.float32),
        scratch_shapes=[pltpu.VMEM((n_chain, tq, 2 * C_QK), MXU_DTYPE), stat,
                        pltpu.VMEM((n_chain, tq, 2 * LANES), jnp.float32)],
        compiler_params=_params(2),
        name="diff_attn",
    )(q, k, v, consts)


SUBLANES = 8
COUNT_CHAINS = 4
PACKED_ROWS = 16
HALF_MIN = -2 ** 15
IDX_GROUP = 4


def _tree_sum(parts):
    while len(parts) > 1:
        parts = [a + b for a, b in zip(parts[0::2], parts[1::2])] + parts[len(parts) & ~1:]
    return parts[0]


def _dsa_kernel(qb_ref, qi_ref, w_ref, ki_ref, kb_ref, vb_ref, o_ref,
                keys_sc, half_sc, qz_sc, m_sc, acc_sc, *, tq, n_sel):
    i = pl.program_id(1)
    tk = tq
    n_tiles = i + 1
    f32 = jnp.float32

    lane = lax.broadcasted_iota(jnp.int32, (tq, LANES), 1)
    for h in range(IDX_HEADS):
        pair = qi_ref[:, (h // 2) * LANES:(h // 2 + 1) * LANES]
        keep = (lane < IDX_DIM) if h % 2 == 0 else (lane >= IDX_DIM)
        qz_sc[h * tq:(h + 1) * tq, :] = jnp.where(keep, pair, jnp.zeros_like(pair))
    w_t = w_ref[...].T

    def score_tile(j, masked):
        rows = pl.ds(pl.multiple_of(j * tk, tk), tk)
        ki = ki_ref[rows, :]
        sc = jnp.zeros((tk, tq), f32)
        for h0 in range(0, IDX_HEADS, IDX_GROUP):
            rel = _dot_nt(ki, qz_sc[h0 * tq:(h0 + IDX_GROUP) * tq, :])
            for g in range(IDX_GROUP):
                sc = sc + jnp.maximum(rel[:, g * tq:(g + 1) * tq], 0.0) * w_t[h0 + g:h0 + g + 1, :]
        if masked:
            kc = (j * tk + lax.broadcasted_iota(jnp.int32, (tk, tq), 0)) // CHUNK
            qc = (i * tq + lax.broadcasted_iota(jnp.int32, (tk, tq), 1)) // CHUNK
            sc = jnp.where(kc <= qc, sc, NEG_INF)
        bits = lax.bitcast_convert_type(sc, jnp.int32)
        key = jnp.where(bits < 0, INT_MIN - bits, bits)
        key = jnp.where(sc > 0.5 * NEG_INF, key, INT_MIN)
        keys_sc[j] = key
        half_sc[j] = (key >> 16).astype(jnp.int16)

    def score_body(j, carry):
        score_tile(j, False)
        return carry

    lax.fori_loop(0, n_tiles - 1, score_body, 0)
    score_tile(n_tiles - 1, True)

    def count_ge(t):
        def body(j, accs):
            accs = list(accs)
            for n, r in enumerate(range(0, tk, SUBLANES)):
                hit = jnp.where(keys_sc[j, r:r + SUBLANES, :] >= t, 1.0, 0.0)
                accs[n % COUNT_CHAINS] = accs[n % COUNT_CHAINS] + hit
            return tuple(accs)
        zero = jnp.zeros((SUBLANES, tq), f32)
        accs = lax.fori_loop(0, n_tiles, body, (zero,) * COUNT_CHAINS)
        return jnp.sum(_tree_sum(list(accs)), axis=0, keepdims=True)

    def count16(ref, t, strict):
        t16 = t.astype(jnp.int16)
        one, zero = jnp.ones((), jnp.int16), jnp.zeros((), jnp.int16)

        def body(j, accs):
            accs = list(accs)
            for n, r in enumerate(range(0, tk, PACKED_ROWS)):
                x = ref[j, r:r + PACKED_ROWS, :]
                hit = jnp.where((x > t16) if strict else (x >= t16), one, zero)
                accs[n % COUNT_CHAINS] = accs[n % COUNT_CHAINS] + hit
            return tuple(accs)

        accs = lax.fori_loop(0, n_tiles, body, (jnp.zeros((PACKED_ROWS, tq), jnp.int16),) * COUNT_CHAINS)
        return jnp.sum(_tree_sum(list(accs)).astype(jnp.int32), axis=0, keepdims=True)

    def kth_largest16(ref, k_need):
        def step(it, t):
            cand = t + lax.shift_left(jnp.int32(1), jnp.int32(15) - it)
            return jnp.where(count16(ref, cand, False) >= k_need, cand, t)
        return lax.fori_loop(0, 16, step, jnp.full((1, tq), HALF_MIN, jnp.int32))

    thr_hi = kth_largest16(half_sc, n_sel)
    k_lo = n_sel - count16(half_sc, thr_hi, True)
    thr_hi16 = thr_hi.astype(jnp.int16)

    def lo_body(j, carry):
        lo = ((keys_sc[j] & 0xFFFF) + HALF_MIN).astype(jnp.int16)
        half_sc[j] = jnp.where(half_sc[j] == thr_hi16, lo, jnp.int16(HALF_MIN))
        return carry

    lax.fori_loop(0, n_tiles, lo_body, 0)
    thr_lo = kth_largest16(half_sc, k_lo)
    thr = thr_hi * 65536 + (thr_lo - HALF_MIN)
    thr = jnp.maximum(thr, INT_MIN + 1)
    k_f = float(n_sel)
    cnt_ge = count_ge(thr)

    @pl.when(jnp.max(cnt_ge) > k_f)
    def _():
        keep_n = k_f - count_ge(thr + 1)
        tri = (lax.broadcasted_iota(jnp.int32, (tk, tk), 0)
               >= lax.broadcasted_iota(jnp.int32, (tk, tk), 1)).astype(MXU_DTYPE)

        def body(j, carry):
            kt = keys_sc[j]
            eq = kt == thr
            rank = carry + _dot(tri, jnp.where(eq, 1.0, 0.0).astype(MXU_DTYPE))
            keys_sc[j] = jnp.where(eq & (rank > keep_n), INT_MIN, kt)
            return rank[tk - 1:tk, :]

        lax.fori_loop(0, n_tiles, body, jnp.zeros((1, tq), f32))

    def bias_body(j, carry):
        bias = jnp.where(keys_sc[j] >= thr, 0.0, NEG_INF).T
        keys_sc[j] = lax.bitcast_convert_type(bias, jnp.int32)
        return carry

    lax.fori_loop(0, n_tiles, bias_body, 0)
    _softmax_init(m_sc, acc_sc)

    def attn_body(j, carry):
        rows = pl.ds(pl.multiple_of(j * tk, tk), tk)
        bias = lax.bitcast_convert_type(keys_sc[j], f32)
        for h in range(B_HEADS):
            sl = slice(h * HEAD_DIM, (h + 1) * HEAD_DIM)
            s = _dot_nt(qb_ref[:, sl], kb_ref[rows, sl]) + bias
            _online_softmax_step(s, vb_ref[rows, sl], m_sc.at[h], acc_sc.at[h])
        return carry

    lax.fori_loop(0, n_tiles, attn_body, 0)
    for h in range(B_HEADS):
        o_ref[:, h * HEAD_DIM:(h + 1) * HEAD_DIM] = _softmax_finish(acc_sc.at[h])


def _dsa_attn(qb, qi, w, ki, kb, vb, n_sel):
    b, s, _ = qb.shape
    tq = min(s, 512)
    blk = lambda width: pl.BlockSpec((None, tq, width), lambda bi, i: (bi, i, 0))
    res = lambda width: pl.BlockSpec((None, s, width), lambda bi, i: (bi, 0, 0), pipeline_mode=pl.Buffered(1))
    return pl.pallas_call(
        functools.partial(_dsa_kernel, tq=tq, n_sel=n_sel),
        grid=(b, s // tq),
        in_specs=[blk(B_WIDTH), blk(IDX_HEADS * IDX_DIM), blk(LANES), res(LANES), res(B_WIDTH), res(B_WIDTH)],
        out_specs=blk(B_WIDTH),
        out_shape=jax.ShapeDtypeStruct((b, s, B_WIDTH), jnp.float32),
        scratch_shapes=[pltpu.VMEM((s // tq, tq, tq), jnp.int32),
                        pltpu.VMEM((s // tq, tq, tq), jnp.int16),
                        pltpu.VMEM((IDX_HEADS * tq, LANES), MXU_DTYPE),
                        pltpu.VMEM((B_HEADS, tq, LANES), jnp.float32),
                        pltpu.VMEM((B_HEADS, tq, 2 * LANES), jnp.float32)],
        compiler_params=_params(2),
        name="dsa_attn",
    )(qb, qi, w, ki, kb, vb)


def _out_kernel(oa_ref, ob_ref, oc_ref, g_ref, h_ref, p_ref, wo_ref, wpg_ref, wple_ref, gn_ref,
                *out_refs, emit_h):
    g = g_ref[...]
    sg = g * _sigmoid(g)
    h1 = h_ref[...]
    col = 0
    for o_ref in (oa_ref, ob_ref, oc_ref):
        width = o_ref.shape[1]
        mixed = (o_ref[...] * sg[:, col:col + width]).astype(MXU_DTYPE)
        h1 = h1 + _dot(mixed, wo_ref[col:col + width, :])
        col += width
    gate = _sigmoid(_dot(h1.astype(MXU_DTYPE), wpg_ref[...]))
    h2 = h1 + _dot(p_ref[...].astype(MXU_DTYPE), wple_ref[...]) * gate
    if emit_h:
        out_refs[0][...] = h2
    out_refs[-1][...] = _rms(h2, gn_ref[...]).astype(out_refs[-1].dtype)


def _out_proj(oa, ob, oc, g, h, p, wo, wpg, wple, layer, gn, emit_h):
    n, d = h.shape
    tm = min(n, 256)
    row = lambda a: pl.BlockSpec((tm, a.shape[1]), lambda i: (i, 0))
    const = lambda a: _layer_spec(a, layer, pipeline_mode=pl.Buffered(1))
    out_specs = [pl.BlockSpec((tm, d), lambda i: (i, 0))]
    out_shape = [jax.ShapeDtypeStruct((n, d), MXU_DTYPE if emit_h else jnp.float32)]
    if emit_h:
        out_specs = [pl.BlockSpec((tm, d), lambda i: (i, 0))] + out_specs
        out_shape = [jax.ShapeDtypeStruct((n, d), jnp.float32)] + out_shape
    return pl.pallas_call(
        functools.partial(_out_kernel, emit_h=emit_h),
        grid=(n // tm,),
        in_specs=[row(oa), row(ob), row(oc), row(g), row(h),
                  pl.BlockSpec((None, tm, p.shape[2]), lambda i: (layer, i, 0)),
                  const(wo), const(wpg), const(wple), pl.BlockSpec((1, d), lambda i: (0, 0))],
        out_specs=out_specs,
        out_shape=out_shape,
        compiler_params=_params(1),
        name="out_proj",
    )(oa, ob, oc, g, h, p, wo, wpg, wple, gn)


def _in_cols(name):
    k = IN_NAMES.index(name)
    start = sum(IN_SPLITS[:k])
    return start, start + IN_SPLITS[k]


_W_BLOCKS = (("c_q", "c_kv", "k_rope", LANES - A_ROPE),
             ("g_a", "g_b", "g_c"),
             ("q_b", "k_b", "v_b", "q_idx", "k_idx", "k_idx", "w_idx", LANES - IDX_HEADS),
             ("q_c", "k_c", "v_c"))


def _block_width(parts):
    return sum(p if isinstance(p, int) else IN_SPLITS[IN_NAMES.index(p)] for p in parts)


def _relayout_kernel(wt_ref, *out_refs):
    for parts, o_ref in zip(_W_BLOCKS, out_refs):
        row, pieces = 0, []
        for part in parts:
            if isinstance(part, int):
                pieces.append(jnp.zeros((part, wt_ref.shape[1]), jnp.float32))
            else:
                a, b = _in_cols(part)
                pieces.append(wt_ref[a:b, :])
            rows = sum(x.shape[0] for x in pieces)
            if rows % PACKED_ROWS == 0:
                o_ref[row:row + rows, :] = jnp.concatenate(pieces, axis=0).astype(o_ref.dtype)
                row, pieces = row + rows, []
        assert not pieces


def _relayout_w_in(w_in):
    depth, d, d_in = w_in.shape
    wt = jnp.transpose(w_in, (0, 2, 1))
    tc = 256
    widths = [_block_width(parts) for parts in _W_BLOCKS]
    return pl.pallas_call(
        _relayout_kernel,
        grid=(depth, d // tc),
        in_specs=[pl.BlockSpec((None, d_in, tc), lambda l, c: (l, 0, c))],
        out_specs=[pl.BlockSpec((None, w, tc), lambda l, c: (l, 0, c)) for w in widths],
        out_shape=[jax.ShapeDtypeStruct((depth, w, d), MXU_DTYPE) for w in widths],
        compiler_params=_params(2),
        name="relayout_w_in",
    )(wt)


_SEG_G = ((0, D_MIX, None, None),)
_SEG_B = ((0, B_WIDTH, PAT_DSA, HEAD_DIM ** -0.5 * LOG2E), (B_WIDTH, B_WIDTH, PAT_DSA, None),
          (2 * B_WIDTH, B_WIDTH, None, None), (3 * B_WIDTH, IDX_HEADS * IDX_DIM, PAT_SMALL, None),
          (3 * B_WIDTH + IDX_HEADS * IDX_DIM, LANES, PAT_SMALL, None),
          (3 * B_WIDTH + IDX_HEADS * IDX_DIM + LANES, LANES, None, None))
_SEG_C = ((0, C_WIDTH, PAT_SMALL, C_QK ** -0.5 * LOG2E), (C_WIDTH, C_WIDTH, PAT_SMALL, None),
          (2 * C_WIDTH, C_WIDTH, None, None))


def kernel(x, p, positions, w_in, w_uq, w_ukv, w_o, norm_g, q_norm_g, kv_norm_g,
           lam_q1, lam_k1, lam_q2, lam_k2, subln_g, w_ple, w_pg, final_g):
    b, s, d = x.shape
    n = b * s
    f32 = jnp.float32
    n_sel = min(TOPK_MAX, s // 4)

    w_a, w_g, w_b, w_c = _relayout_w_in(w_in)
    uq = w_uq.reshape(DEPTH, A_Q_RANK, A_HEADS, A_NOPE + A_ROPE)
    uq = jnp.pad(uq, ((0, 0), (0, 0), (0, 0), (0, A_QK_PAD - A_NOPE - A_ROPE)))
    uq = uq.reshape(DEPTH, A_Q_RANK, A_HEADS * A_QK_PAD).astype(MXU_DTYPE)
    ukv = w_ukv.reshape(DEPTH, A_KV_RANK, A_HEADS, A_NOPE + A_V)
    ukv = jnp.concatenate([ukv[..., :A_NOPE].reshape(DEPTH, A_KV_RANK, A_WIDTH),
                           ukv[..., A_NOPE:].reshape(DEPTH, A_KV_RANK, A_WIDTH)], axis=-1).astype(MXU_DTYPE)
    w_o_c = w_o.astype(MXU_DTYPE)
    w_pg_c = w_pg.astype(MXU_DTYPE)
    w_ple_c = w_ple.astype(MXU_DTYPE)
    pad_lanes = lambda v: jnp.pad(v, (0, LANES - v.shape[0]))

    tabs = _rope_tables(positions.astype(f32).reshape(n, 1))
    h = x.reshape(n, d)
    u = _norm(h, norm_g[0], MXU_DTYPE)
    out = None
    for i in range(DEPTH):
        q_a, k_a, v_a = _mla_proj(u, w_a, i, q_norm_g[i].reshape(1, -1), kv_norm_g[i].reshape(1, -1),
                                  uq, ukv, tabs)
        (g,) = _proj(u, w_g, i, None, _SEG_G, (f32,), "proj_gates")
        q_b, k_b, v_b, q_i, k_i, w_i = _proj(u, w_b, i, tabs, _SEG_B,
                                             (MXU_DTYPE,) * 5 + (f32,), "proj_dsa")
        q_c, k_c, v_c = _proj(u, w_c, i, tabs, _SEG_C, (MXU_DTYPE,) * 3, "proj_diff")

        r3 = lambda a: a.reshape(b, s, a.shape[-1])
        o_a = _mla_attn(r3(q_a), r3(k_a), r3(v_a))
        o_b = _dsa_attn(r3(q_b), r3(q_i), r3(w_i), r3(k_i), r3(k_b), r3(v_b), n_sel)
        lam_init = 0.8 - 0.6 * math.exp(-0.3 * i)
        consts = jnp.stack([pad_lanes(lam_q1[i]), pad_lanes(lam_k1[i]), pad_lanes(lam_q2[i]),
                            pad_lanes(lam_k2[i]), jnp.full((LANES,), lam_init, f32), subln_g[i],
                            jnp.zeros((LANES,), f32), jnp.zeros((LANES,), f32)]).astype(f32)
        o_c = _diff_attn(r3(q_c), r3(k_c), r3(v_c), consts)

        last = i == DEPTH - 1
        gn = (final_g if last else norm_g[i + 1]).reshape(1, d)
        res = _out_proj(o_a.reshape(n, -1), o_b.reshape(n, -1), o_c.reshape(n, -1), g, h,
                        p.reshape(DEPTH, n, PLE_DIM), w_o_c, w_pg_c, w_ple_c, i, gn, emit_h=not last)
        if last:
            out = res[0]
        else:
            h, u = res
    return out.reshape(b, s, d)
```

```python
import functools
import math

import jax
import jax.numpy as jnp
import numpy as np
from jax import lax
from jax.experimental import pallas as pl
from jax.experimental.pallas import tpu as pltpu

D_MODEL = 2048
DEPTH = 4
CHUNK = 64
ROPE_THETA = 500000.0
NORM_EPS = 1e-6
NEG_INF = -1e30
PLE_DIM = 256
HEAD_DIM = 128

A_HEADS = 6
A_Q_RANK = 384
A_KV_RANK = 256
A_NOPE = 128
A_ROPE = 64
A_V = 128
A_WIDTH = A_HEADS * A_V
B_HEADS = 5
B_WIDTH = B_HEADS * HEAD_DIM
IDX_HEADS = 8
IDX_DIM = 64
TOPK_MAX = 256
C_HEADS = 5
C_QK = 64
C_V = 128
C_WIDTH = C_HEADS * C_V
D_MIX = A_WIDTH + B_WIDTH + C_WIDTH

IN_NAMES = ("c_q", "c_kv", "k_rope", "g_a", "q_b", "k_b", "v_b", "q_idx", "k_idx", "w_idx", "g_b",
            "q_c", "k_c", "v_c", "g_c")
IN_SPLITS = (A_Q_RANK, A_KV_RANK, A_ROPE, A_WIDTH,
             B_WIDTH, B_WIDTH, B_WIDTH, IDX_HEADS * IDX_DIM, IDX_DIM, IDX_HEADS, B_WIDTH,
             C_HEADS * 2 * C_QK, C_HEADS * 2 * C_QK, C_WIDTH, C_WIDTH)

LANES = 128
A_QK_PAD = 256
INT_MIN = -2 ** 31
LOG2E = math.log2(math.e)
MXU_DTYPE = jnp.bfloat16
VMEM_LIMIT = 56 * 1024 * 1024

PROJ_ROWS = 512
OUT_ROWS = 256
ATTN_TILE = 512
ROPE_ROWS = 1024
RELAYOUT_COLS = 256

ROPE_PATTERNS = ((A_ROPE, 64), (HEAD_DIM // 4, 128), (IDX_DIM // 4, 64))
PAT_MLA, PAT_DSA, PAT_SMALL = 0, 1, 2

_NT = (((1,), (1,)), ((), ()))


def _params(n_axes):
    return pltpu.CompilerParams(dimension_semantics=("arbitrary",) * n_axes,
                                vmem_limit_bytes=VMEM_LIMIT)


def _dot(a, b):
    return jnp.dot(a, b, preferred_element_type=jnp.float32)


def _dot_nt(a, b):
    return lax.dot_general(a, b, _NT, preferred_element_type=jnp.float32)


def _sigmoid(x):
    return 1.0 / (1.0 + jnp.exp(-x))


def _rms(x, g):
    return x * lax.rsqrt(jnp.mean(x * x, axis=-1, keepdims=True) + NORM_EPS) * g


def _rope_pattern_rows():
    freq = []
    m_up = np.zeros((3, LANES), np.float32)
    m_dn = np.zeros((3, LANES), np.float32)
    for p, (n_rot, period) in enumerate(ROPE_PATTERNS):
        half = n_rot // 2
        inv = 1.0 / (ROPE_THETA ** (jnp.arange(half, dtype=jnp.float32) * (2.0 / n_rot)))
        one = jnp.concatenate([inv, inv, jnp.zeros((period - n_rot,), jnp.float32)])
        freq.append(jnp.tile(one, LANES // period))
        for lane in range(LANES):
            m = lane % period
            m_up[p, lane] = float(m < half)
            m_dn[p, lane] = float(half <= m < n_rot)
    return jnp.stack(freq), jnp.asarray(m_up), jnp.asarray(m_dn)


def _rope_tab_kernel(pos_ref, f_ref, mu_ref, md_ref, c_ref, sa_ref, sb_ref):
    pos = pos_ref[...]
    for p in range(3):
        ang = pos * f_ref[p:p + 1, :]
        sin = jnp.sin(ang)
        c_ref[p] = jnp.cos(ang)
        sa_ref[p] = -sin * mu_ref[p:p + 1, :]
        sb_ref[p] = sin * md_ref[p:p + 1, :]


def _rope_tables(pos_f32):
    n = pos_f32.shape[0]
    tm = min(n, ROPE_ROWS)
    freq, m_up, m_dn = _rope_pattern_rows()
    row = pl.BlockSpec((3, LANES), lambda i: (0, 0))
    tab = pl.BlockSpec((3, tm, LANES), lambda i: (0, i, 0))
    shape = jax.ShapeDtypeStruct((3, n, LANES), jnp.float32)
    return pl.pallas_call(
        _rope_tab_kernel,
        grid=(n // tm,),
        in_specs=[pl.BlockSpec((tm, 1), lambda i: (i, 0)), row, row, row],
        out_specs=[tab, tab, tab],
        out_shape=[shape, shape, shape],
        compiler_params=_params(1),
        name="rope_tables",
    )(pos_f32, freq, m_up, m_dn)


def _rope128(x, c, sa, sb, half):
    return x * c + pltpu.roll(x, LANES - half, 1) * sa + pltpu.roll(x, half, 1) * sb


def _norm_kernel(x_ref, g_ref, o_ref):
    o_ref[...] = _rms(x_ref[...], g_ref[...]).astype(o_ref.dtype)


def _norm(x, g, out_dtype):
    n, d = x.shape
    tm = min(n, PROJ_ROWS)
    return pl.pallas_call(
        _norm_kernel,
        grid=(n // tm,),
        in_specs=[pl.BlockSpec((tm, d), lambda i: (i, 0)), pl.BlockSpec((1, d), lambda i: (0, 0))],
        out_specs=pl.BlockSpec((tm, d), lambda i: (i, 0)),
        out_shape=jax.ShapeDtypeStruct((n, d), out_dtype),
        compiler_params=_params(1),
        name="rms_norm",
    )(x, g.reshape(1, d))


def _proj_kernel(*refs, segs, with_tables):
    u_ref, w_ref = refs[0], refs[1]
    if with_tables:
        c_ref, sa_ref, sb_ref = refs[2:5]
        outs = refs[5:]
    else:
        outs = refs[2:]
    u = u_ref[...]
    for (c0, width, pat, scale), o_ref in zip(segs, outs):
        z = _dot_nt(u, w_ref[c0:c0 + width, :])
        if pat is None and scale is None:
            o_ref[...] = z.astype(o_ref.dtype)
            continue
        for ch in range(width // LANES):
            sl = slice(ch * LANES, (ch + 1) * LANES)
            x = z[:, sl]
            if pat is not None:
                x = _rope128(x, c_ref[pat], sa_ref[pat], sb_ref[pat], ROPE_PATTERNS[pat][0] // 2)
            if scale is not None:
                x = x * scale
            o_ref[:, sl] = x.astype(o_ref.dtype)


def _layer_spec(stacked, layer, **kwargs):
    zeros = (0,) * (stacked.ndim - 1)
    return pl.BlockSpec((None,) + stacked.shape[1:], lambda i: (layer,) + zeros, **kwargs)


def _proj(u, w, layer, tabs, segs, out_dtypes, name):
    n, d = u.shape
    tm = min(n, PROJ_ROWS)
    in_specs = [pl.BlockSpec((tm, d), lambda i: (i, 0)), _layer_spec(w, layer)]
    args = [u, w]
    if tabs is not None:
        in_specs += [pl.BlockSpec((3, tm, LANES), lambda i: (0, i, 0))] * 3
        args += list(tabs)
    out_specs = [pl.BlockSpec((tm, s[1]), lambda i: (i, 0)) for s in segs]
    out_shape = [jax.ShapeDtypeStruct((n, s[1]), dt) for s, dt in zip(segs, out_dtypes)]
    return pl.pallas_call(
        functools.partial(_proj_kernel, segs=tuple(segs), with_tables=tabs is not None),
        grid=(n // tm,),
        in_specs=in_specs,
        out_specs=out_specs,
        out_shape=out_shape,
        compiler_params=_params(1),
        name=name,
    )(*args)


def _mla_proj_kernel(u_ref, w_ref, qg_ref, kvg_ref, wuq_ref, wukv_ref, c_ref, sa_ref, sb_ref,
                     q_ref, k_ref, v_ref):
    z = _dot_nt(u_ref[...], w_ref[...])
    c, sa, sb = c_ref[PAT_MLA], sa_ref[PAT_MLA], sb_ref[PAT_MLA]
    half = A_ROPE // 2
    scale = (A_NOPE + A_ROPE) ** -0.5 * LOG2E
    c_q = z[:, :A_Q_RANK]
    c_kv = z[:, A_Q_RANK:A_Q_RANK + A_KV_RANK]
    k_rope = z[:, A_Q_RANK + A_KV_RANK:]
    q = _dot(_rms(c_q, qg_ref[...]).astype(MXU_DTYPE), wuq_ref[...])
    kv = _dot(_rms(c_kv, kvg_ref[...]).astype(MXU_DTYPE), wukv_ref[...])
    k_pe = _rope128(k_rope, c, sa, sb, half).astype(k_ref.dtype)
    for h in range(A_HEADS):
        o = h * A_QK_PAD
        q_ref[:, o:o + A_NOPE] = (q[:, o:o + A_NOPE] * scale).astype(q_ref.dtype)
        q_rot = _rope128(q[:, o + A_NOPE:o + A_QK_PAD], c, sa, sb, half)
        q_ref[:, o + A_NOPE:o + A_QK_PAD] = (q_rot * scale).astype(q_ref.dtype)
        k_ref[:, o:o + A_NOPE] = kv[:, h * A_NOPE:(h + 1) * A_NOPE].astype(k_ref.dtype)
        k_ref[:, o + A_NOPE:o + A_QK_PAD] = k_pe
    v_ref[...] = kv[:, A_WIDTH:].astype(v_ref.dtype)


def _mla_proj(u, w, layer, qg, kvg, wuq, wukv, tabs):
    n, d = u.shape
    tm = min(n, PROJ_ROWS)
    full = lambda a: pl.BlockSpec(a.shape, lambda i: (0,) * a.ndim)
    tab = pl.BlockSpec((3, tm, LANES), lambda i: (0, i, 0))
    qk = A_HEADS * A_QK_PAD
    return pl.pallas_call(
        _mla_proj_kernel,
        grid=(n // tm,),
        in_specs=[pl.BlockSpec((tm, d), lambda i: (i, 0)), _layer_spec(w, layer), full(qg), full(kvg),
                  _layer_spec(wuq, layer), _layer_spec(wukv, layer), tab, tab, tab],
        out_specs=[pl.BlockSpec((tm, qk), lambda i: (i, 0)), pl.BlockSpec((tm, qk), lambda i: (i, 0)),
                   pl.BlockSpec((tm, A_WIDTH), lambda i: (i, 0))],
        out_shape=[jax.ShapeDtypeStruct((n, qk), MXU_DTYPE), jax.ShapeDtypeStruct((n, qk), MXU_DTYPE),
                   jax.ShapeDtypeStruct((n, A_WIDTH), MXU_DTYPE)],
        compiler_params=_params(1),
        name="mla_proj",
    )(u, w, qg, kvg, wuq, wukv, *tabs)


def _diag_mask(t):
    qc = lax.broadcasted_iota(jnp.int32, (t, t), 0) // CHUNK
    kc = lax.broadcasted_iota(jnp.int32, (t, t), 1) // CHUNK
    return kc <= qc


def _softmax_init(m_ref, acc_ref):
    m_ref[...] = jnp.full(m_ref.shape, NEG_INF, jnp.float32)
    acc_ref[...] = jnp.zeros(acc_ref.shape, jnp.float32)


def _online_softmax_step(s, v, m_ref, acc_ref):
    chunks = [s[:, c * LANES:(c + 1) * LANES] for c in range(s.shape[1] // LANES)]
    cmax = functools.reduce(jnp.maximum, chunks)
    m_prev = m_ref[...]
    m_new = jnp.maximum(m_prev, jnp.max(cmax, axis=-1, keepdims=True))
    alpha = jnp.exp2(m_prev - m_new)
    p = jnp.concatenate([jnp.exp2(c - m_new).astype(v.dtype) for c in chunks], axis=1)
    v_ones = jnp.concatenate([v, jnp.ones_like(v)], axis=1)
    acc_ref[...] = jnp.concatenate([alpha, alpha], axis=1) * acc_ref[...] + _dot(p, v_ones)
    m_ref[...] = m_new


def _softmax_finish(acc_ref):
    acc = acc_ref[...]
    return acc[:, :LANES] / acc[:, LANES:]


MLA_HEADS_PER_STEP = 6


def _mla_attn_kernel(q_ref, k_ref, v_ref, o_ref, m_sc, acc_sc, *, tq):
    i = pl.program_id(2)
    _softmax_init(m_sc, acc_sc)

    def tile(j, masked):
        rows = pl.ds(pl.multiple_of(j * tq, tq), tq)
        for h in range(MLA_HEADS_PER_STEP):
            qk = slice(h * A_QK_PAD, (h + 1) * A_QK_PAD)
            s = _dot_nt(q_ref[:, qk], k_ref[rows, qk])
            if masked:
                s = jnp.where(_diag_mask(tq), s, NEG_INF)
            _online_softmax_step(s, v_ref[rows, h * A_V:(h + 1) * A_V], m_sc.at[h], acc_sc.at[h])

    def body(j, carry):
        tile(j, False)
        return carry

    lax.fori_loop(0, i, body, 0)
    tile(i, True)
    for h in range(MLA_HEADS_PER_STEP):
        o_ref[:, h * A_V:(h + 1) * A_V] = _softmax_finish(acc_sc.at[h])


def _mla_attn(q, k, v):
    b, s, _ = q.shape
    tq = min(s, ATTN_TILE)
    hp = MLA_HEADS_PER_STEP
    stat = pltpu.VMEM((hp, tq, LANES), jnp.float32)
    return pl.pallas_call(
        functools.partial(_mla_attn_kernel, tq=tq),
        grid=(b, A_HEADS // hp, s // tq),
        in_specs=[pl.BlockSpec((None, tq, hp * A_QK_PAD), lambda bi, h, i: (bi, i, h)),
                  pl.BlockSpec((None, s, hp * A_QK_PAD), lambda bi, h, i: (bi, 0, h),
                               pipeline_mode=pl.Buffered(1)),
                  pl.BlockSpec((None, s, hp * A_V), lambda bi, h, i: (bi, 0, h),
                               pipeline_mode=pl.Buffered(1))],
        out_specs=pl.BlockSpec((None, tq, hp * A_V), lambda bi, h, i: (bi, i, h)),
        out_shape=jax.ShapeDtypeStruct((b, s, A_WIDTH), jnp.float32),
        scratch_shapes=[stat, pltpu.VMEM((hp, tq, 2 * LANES), jnp.float32)],
        compiler_params=_params(3),
        name="mla_attn",
    )(q, k, v)


def _diff_attn_kernel(q_ref, k_ref, v_ref, c_ref, o_ref, qs_sc, m_sc, acc_sc, *, tq):
    i = pl.program_id(1)
    lane = lax.broadcasted_iota(jnp.int32, (tq, 2 * C_QK), 1)
    for h in range(C_HEADS):
        q = q_ref[:, h * 2 * C_QK:(h + 1) * 2 * C_QK]
        qs_sc[2 * h] = jnp.where(lane < C_QK, q, jnp.zeros_like(q))
        qs_sc[2 * h + 1] = jnp.where(lane >= C_QK, q, jnp.zeros_like(q))
    _softmax_init(m_sc, acc_sc)

    def tile(j, masked):
        rows = pl.ds(pl.multiple_of(j * tq, tq), tq)
        for h in range(C_HEADS):
            k = k_ref[rows, h * 2 * C_QK:(h + 1) * 2 * C_QK]
            v = v_ref[rows, h * C_V:(h + 1) * C_V]
            for t in range(2):
                n = 2 * h + t
                s = _dot_nt(qs_sc[n], k)
                if masked:
                    s = jnp.where(_diag_mask(tq), s, NEG_INF)
                _online_softmax_step(s, v, m_sc.at[n], acc_sc.at[n])

    def body(j, carry):
        tile(j, False)
        return carry

    lax.fori_loop(0, i, body, 0)
    tile(i, True)

    c = c_ref[...]
    lam_init = c[4:5, 0:1]
    lam = (jnp.exp(jnp.sum(c[0:1] * c[1:2], axis=-1, keepdims=True))
           - jnp.exp(jnp.sum(c[2:3] * c[3:4], axis=-1, keepdims=True)) + lam_init)
    for h in range(C_HEADS):
        o = _softmax_finish(acc_sc.at[2 * h]) - lam * _softmax_finish(acc_sc.at[2 * h + 1])
        o_ref[:, h * C_V:(h + 1) * C_V] = _rms(o, c[5:6]) * (1.0 - lam_init)


def _diff_attn(q, k, v, consts):
    b, s, _ = q.shape
    tq = min(s, ATTN_TILE)
    n_chain = 2 * C_HEADS
    stat = pltpu.VMEM((n_chain, tq, LANES), jnp.float32)
    resident = lambda width: pl.BlockSpec((None, s, width), lambda bi, i: (bi, 0, 0))
    return pl.pallas_call(
        functools.partial(_diff_attn_kernel, tq=tq),
        grid=(b, s // tq),
        in_specs=[pl.BlockSpec((None, tq, C_WIDTH), lambda bi, i: (bi, i, 0)),
                  resident(C_WIDTH), resident(C_WIDTH),
                  pl.BlockSpec((8, LANES), lambda bi, i: (0, 0))],
        out_specs=pl.BlockSpec((None, tq, C_WIDTH), lambda bi, i: (bi, i, 0)),
        out_shape=jax.ShapeDtypeStruct((b, s, C_WIDTH), jnp.float32),
        scratch_shapes=[pltpu.VMEM((n_chain, tq, 2 * C_QK), MXU_DTYPE), stat,
                        pltpu.VMEM((n_chain, tq, 2 * LANES), jnp.float32)],
        compiler_params=_params(2),
        name="diff_attn",
    )(q, k, v, consts)


SUBLANES = 8
COUNT_CHAINS = 4
PACKED_ROWS = 16
HALF_BITS = 16
HALF_MIN = -2 ** (HALF_BITS - 1)
IDX_GROUP = 4


def _tree_sum(parts):
    while len(parts) > 1:
        parts = [a + b for a, b in zip(parts[0::2], parts[1::2])] + parts[len(parts) & ~1:]
    return parts[0]


def _dsa_kernel(qb_ref, qi_ref, w_ref, ki_ref, kb_ref, vb_ref, o_ref,
                keys_sc, half_sc, qz_sc, m_sc, acc_sc, *, tq, n_sel):
    i = pl.program_id(1)
    tk = tq
    n_tiles = i + 1
    f32 = jnp.float32

    lane = lax.broadcasted_iota(jnp.int32, (tq, LANES), 1)
    for h in range(IDX_HEADS):
        pair = qi_ref[:, (h // 2) * LANES:(h // 2 + 1) * LANES]
        keep = (lane < IDX_DIM) if h % 2 == 0 else (lane >= IDX_DIM)
        qz_sc[h * tq:(h + 1) * tq, :] = jnp.where(keep, pair, jnp.zeros_like(pair))
    w_t = w_ref[...].T

    def score_tile(j, masked):
        rows = pl.ds(pl.multiple_of(j * tk, tk), tk)
        ki = ki_ref[rows, :]
        sc = jnp.zeros((tk, tq), f32)
        for h0 in range(0, IDX_HEADS, IDX_GROUP):
            rel = _dot_nt(ki, qz_sc[h0 * tq:(h0 + IDX_GROUP) * tq, :])
            for g in range(IDX_GROUP):
                sc = sc + jnp.maximum(rel[:, g * tq:(g + 1) * tq], 0.0) * w_t[h0 + g:h0 + g + 1, :]
        if masked:
            kc = (j * tk + lax.broadcasted_iota(jnp.int32, (tk, tq), 0)) // CHUNK
            qc = (i * tq + lax.broadcasted_iota(jnp.int32, (tk, tq), 1)) // CHUNK
            sc = jnp.where(kc <= qc, sc, NEG_INF)
        bits = lax.bitcast_convert_type(sc, jnp.int32)
        key = jnp.where(bits < 0, INT_MIN - bits, bits)
        key = jnp.where(sc > 0.5 * NEG_INF, key, INT_MIN)
        keys_sc[j] = key
        half_sc[j] = (key >> HALF_BITS).astype(jnp.int16)

    def score_body(j, carry):
        score_tile(j, False)
        return carry

    lax.fori_loop(0, n_tiles - 1, score_body, 0)
    score_tile(n_tiles - 1, True)

    def count_ge(t):
        def body(j, accs):
            accs = list(accs)
            for n, r in enumerate(range(0, tk, SUBLANES)):
                hit = jnp.where(keys_sc[j, r:r + SUBLANES, :] >= t, 1.0, 0.0)
                accs[n % COUNT_CHAINS] = accs[n % COUNT_CHAINS] + hit
            return tuple(accs)
        zero = jnp.zeros((SUBLANES, tq), f32)
        accs = lax.fori_loop(0, n_tiles, body, (zero,) * COUNT_CHAINS)
        return jnp.sum(_tree_sum(list(accs)), axis=0, keepdims=True)

    def count16(ref, t, strict):
        t16 = t.astype(jnp.int16)
        one, zero = jnp.ones((), jnp.int16), jnp.zeros((), jnp.int16)

        def body(j, accs):
            accs = list(accs)
            for n, r in enumerate(range(0, tk, PACKED_ROWS)):
                x = ref[j, r:r + PACKED_ROWS, :]
                hit = jnp.where((x > t16) if strict else (x >= t16), one, zero)
                accs[n % COUNT_CHAINS] = accs[n % COUNT_CHAINS] + hit
            return tuple(accs)

        accs = lax.fori_loop(0, n_tiles, body, (jnp.zeros((PACKED_ROWS, tq), jnp.int16),) * COUNT_CHAINS)
        return jnp.sum(_tree_sum(list(accs)).astype(jnp.int32), axis=0, keepdims=True)

    def kth_largest16(ref, k_need):
        def step(it, t):
            cand = t + lax.shift_left(jnp.int32(1), jnp.int32(HALF_BITS - 1) - it)
            return jnp.where(count16(ref, cand, False) >= k_need, cand, t)
        return lax.fori_loop(0, HALF_BITS, step, jnp.full((1, tq), HALF_MIN, jnp.int32))

    thr_hi = kth_largest16(half_sc, n_sel)
    k_lo = n_sel - count16(half_sc, thr_hi, True)
    thr_hi16 = thr_hi.astype(jnp.int16)

    def lo_body(j, carry):
        lo = ((keys_sc[j] & (2 ** HALF_BITS - 1)) + HALF_MIN).astype(jnp.int16)
        half_sc[j] = jnp.where(half_sc[j] == thr_hi16, lo, jnp.int16(HALF_MIN))
        return carry

    lax.fori_loop(0, n_tiles, lo_body, 0)
    thr_lo = kth_largest16(half_sc, k_lo)
    thr = thr_hi * 2 ** HALF_BITS + (thr_lo - HALF_MIN)
    thr = jnp.maximum(thr, INT_MIN + 1)
    k_f = float(n_sel)
    cnt_ge = count_ge(thr)

    @pl.when(jnp.max(cnt_ge) > k_f)
    def _():
        keep_n = k_f - count_ge(thr + 1)
        tri = (lax.broadcasted_iota(jnp.int32, (tk, tk), 0)
               >= lax.broadcasted_iota(jnp.int32, (tk, tk), 1)).astype(MXU_DTYPE)

        def body(j, carry):
            kt = keys_sc[j]
            eq = kt == thr
            rank = carry + _dot(tri, jnp.where(eq, 1.0, 0.0).astype(MXU_DTYPE))
            keys_sc[j] = jnp.where(eq & (rank > keep_n), INT_MIN, kt)
            return rank[tk - 1:tk, :]

        lax.fori_loop(0, n_tiles, body, jnp.zeros((1, tq), f32))

    def bias_body(j, carry):
        bias = jnp.where(keys_sc[j] >= thr, 0.0, NEG_INF).T
        keys_sc[j] = lax.bitcast_convert_type(bias, jnp.int32)
        return carry

    lax.fori_loop(0, n_tiles, bias_body, 0)
    _softmax_init(m_sc, acc_sc)

    def attn_body(j, carry):
        rows = pl.ds(pl.multiple_of(j * tk, tk), tk)
        bias = lax.bitcast_convert_type(keys_sc[j], f32)
        for h in range(B_HEADS):
            sl = slice(h * HEAD_DIM, (h + 1) * HEAD_DIM)
            s = _dot_nt(qb_ref[:, sl], kb_ref[rows, sl]) + bias
            _online_softmax_step(s, vb_ref[rows, sl], m_sc.at[h], acc_sc.at[h])
        return carry

    lax.fori_loop(0, n_tiles, attn_body, 0)
    for h in range(B_HEADS):
        o_ref[:, h * HEAD_DIM:(h + 1) * HEAD_DIM] = _softmax_finish(acc_sc.at[h])


def _dsa_attn(qb, qi, w, ki, kb, vb, n_sel):
    b, s, _ = qb.shape
    tq = min(s, ATTN_TILE)
    blk = lambda width: pl.BlockSpec((None, tq, width), lambda bi, i: (bi, i, 0))
    res = lambda width: pl.BlockSpec((None, s, width), lambda bi, i: (bi, 0, 0), pipeline_mode=pl.Buffered(1))
    return pl.pallas_call(
        functools.partial(_dsa_kernel, tq=tq, n_sel=n_sel),
        grid=(b, s // tq),
        in_specs=[blk(B_WIDTH), blk(IDX_HEADS * IDX_DIM), blk(LANES), res(LANES), res(B_WIDTH), res(B_WIDTH)],
        out_specs=blk(B_WIDTH),
        out_shape=jax.ShapeDtypeStruct((b, s, B_WIDTH), jnp.float32),
        scratch_shapes=[pltpu.VMEM((s // tq, tq, tq), jnp.int32),
                        pltpu.VMEM((s // tq, tq, tq), jnp.int16),
                        pltpu.VMEM((IDX_HEADS * tq, LANES), MXU_DTYPE),
                        pltpu.VMEM((B_HEADS, tq, LANES), jnp.float32),
                        pltpu.VMEM((B_HEADS, tq, 2 * LANES), jnp.float32)],
        compiler_params=_params(2),
        name="dsa_attn",
    )(qb, qi, w, ki, kb, vb)


def _out_kernel(oa_ref, ob_ref, oc_ref, g_ref, h_ref, p_ref, wo_ref, wpg_ref, wple_ref, gn_ref,
                *out_refs, emit_h):
    g = g_ref[...]
    sg = g * _sigmoid(g)
    h1 = h_ref[...]
    col = 0
    for o_ref in (oa_ref, ob_ref, oc_ref):
        width = o_ref.shape[1]
        mixed = (o_ref[...] * sg[:, col:col + width]).astype(MXU_DTYPE)
        h1 = h1 + _dot(mixed, wo_ref[col:col + width, :])
        col += width
    gate = _sigmoid(_dot(h1.astype(MXU_DTYPE), wpg_ref[...]))
    h2 = h1 + _dot(p_ref[...].astype(MXU_DTYPE), wple_ref[...]) * gate
    if emit_h:
        out_refs[0][...] = h2
    out_refs[-1][...] = _rms(h2, gn_ref[...]).astype(out_refs[-1].dtype)


def _out_proj(oa, ob, oc, g, h, p, wo, wpg, wple, layer, gn, emit_h):
    n, d = h.shape
    tm = min(n, OUT_ROWS)
    row = lambda a: pl.BlockSpec((tm, a.shape[1]), lambda i: (i, 0))
    const = lambda a: _layer_spec(a, layer, pipeline_mode=pl.Buffered(1))
    out_specs = [pl.BlockSpec((tm, d), lambda i: (i, 0))]
    out_shape = [jax.ShapeDtypeStruct((n, d), MXU_DTYPE if emit_h else jnp.float32)]
    if emit_h:
        out_specs = [pl.BlockSpec((tm, d), lambda i: (i, 0))] + out_specs
        out_shape = [jax.ShapeDtypeStruct((n, d), jnp.float32)] + out_shape
    return pl.pallas_call(
        functools.partial(_out_kernel, emit_h=emit_h),
        grid=(n // tm,),
        in_specs=[row(oa), row(ob), row(oc), row(g), row(h),
                  pl.BlockSpec((None, tm, p.shape[2]), lambda i: (layer, i, 0)),
                  const(wo), const(wpg), const(wple), pl.BlockSpec((1, d), lambda i: (0, 0))],
        out_specs=out_specs,
        out_shape=out_shape,
        compiler_params=_params(1),
        name="out_proj",
    )(oa, ob, oc, g, h, p, wo, wpg, wple, gn)


def _in_cols(name):
    k = IN_NAMES.index(name)
    start = sum(IN_SPLITS[:k])
    return start, start + IN_SPLITS[k]


_W_BLOCKS = (("c_q", "c_kv", "k_rope", LANES - A_ROPE),
             ("g_a", "g_b", "g_c"),
             ("q_b", "k_b", "v_b", "q_idx", "k_idx", "k_idx", "w_idx", LANES - IDX_HEADS),
             ("q_c", "k_c", "v_c"))


def _block_width(parts):
    return sum(p if isinstance(p, int) else IN_SPLITS[IN_NAMES.index(p)] for p in parts)


def _relayout_kernel(wt_ref, *out_refs):
    for parts, o_ref in zip(_W_BLOCKS, out_refs):
        row, pieces = 0, []
        for part in parts:
            if isinstance(part, int):
                pieces.append(jnp.zeros((part, wt_ref.shape[1]), jnp.float32))
            else:
                a, b = _in_cols(part)
                pieces.append(wt_ref[a:b, :])
            rows = sum(x.shape[0] for x in pieces)
            if rows % PACKED_ROWS == 0:
                o_ref[row:row + rows, :] = jnp.concatenate(pieces, axis=0).astype(o_ref.dtype)
                row, pieces = row + rows, []
        assert not pieces


def _relayout_w_in(w_in):
    depth, d, d_in = w_in.shape
    wt = jnp.transpose(w_in, (0, 2, 1))
    tc = RELAYOUT_COLS
    widths = [_block_width(parts) for parts in _W_BLOCKS]
    return pl.pallas_call(
        _relayout_kernel,
        grid=(depth, d // tc),
        in_specs=[pl.BlockSpec((None, d_in, tc), lambda l, c: (l, 0, c))],
        out_specs=[pl.BlockSpec((None, w, tc), lambda l, c: (l, 0, c)) for w in widths],
        out_shape=[jax.ShapeDtypeStruct((depth, w, d), MXU_DTYPE) for w in widths],
        compiler_params=_params(2),
        name="relayout_w_in",
    )(wt)


_SEG_G = ((0, D_MIX, None, None),)
_SEG_B = ((0, B_WIDTH, PAT_DSA, HEAD_DIM ** -0.5 * LOG2E), (B_WIDTH, B_WIDTH, PAT_DSA, None),
          (2 * B_WIDTH, B_WIDTH, None, None), (3 * B_WIDTH, IDX_HEADS * IDX_DIM, PAT_SMALL, None),
          (3 * B_WIDTH + IDX_HEADS * IDX_DIM, LANES, PAT_SMALL, None),
          (3 * B_WIDTH + IDX_HEADS * IDX_DIM + LANES, LANES, None, None))
_SEG_C = ((0, C_WIDTH, PAT_SMALL, C_QK ** -0.5 * LOG2E), (C_WIDTH, C_WIDTH, PAT_SMALL, None),
          (2 * C_WIDTH, C_WIDTH, None, None))


def kernel(x, p, positions, w_in, w_uq, w_ukv, w_o, norm_g, q_norm_g, kv_norm_g,
           lam_q1, lam_k1, lam_q2, lam_k2, subln_g, w_ple, w_pg, final_g):
    b, s, d = x.shape
    n = b * s
    f32 = jnp.float32
    n_sel = min(TOPK_MAX, s // 4)

    w_a, w_g, w_b, w_c = _relayout_w_in(w_in)
    uq = w_uq.reshape(DEPTH, A_Q_RANK, A_HEADS, A_NOPE + A_ROPE)
    uq = jnp.pad(uq, ((0, 0), (0, 0), (0, 0), (0, A_QK_PAD - A_NOPE - A_ROPE)))
    uq = uq.reshape(DEPTH, A_Q_RANK, A_HEADS * A_QK_PAD).astype(MXU_DTYPE)
    ukv = w_ukv.reshape(DEPTH, A_KV_RANK, A_HEADS, A_NOPE + A_V)
    ukv = jnp.concatenate([ukv[..., :A_NOPE].reshape(DEPTH, A_KV_RANK, A_WIDTH),
                           ukv[..., A_NOPE:].reshape(DEPTH, A_KV_RANK, A_WIDTH)], axis=-1).astype(MXU_DTYPE)
    w_o_c = w_o.astype(MXU_DTYPE)
    w_pg_c = w_pg.astype(MXU_DTYPE)
    w_ple_c = w_ple.astype(MXU_DTYPE)
    pad_lanes = lambda v: jnp.pad(v, (0, LANES - v.shape[0]))

    tabs = _rope_tables(positions.astype(f32).reshape(n, 1))
    h = x.reshape(n, d)
    u = _norm(h, norm_g[0], MXU_DTYPE)
    out = None
    for i in range(DEPTH):
        q_a, k_a, v_a = _mla_proj(u, w_a, i, q_norm_g[i].reshape(1, -1), kv_norm_g[i].reshape(1, -1),
                                  uq, ukv, tabs)
        (g,) = _proj(u, w_g, i, None, _SEG_G, (f32,), "proj_gates")
        q_b, k_b, v_b, q_i, k_i, w_i = _proj(u, w_b, i, tabs, _SEG_B,
                                             (MXU_DTYPE,) * 5 + (f32,), "proj_dsa")
        q_c, k_c, v_c = _proj(u, w_c, i, tabs, _SEG_C, (MXU_DTYPE,) * 3, "proj_diff")

        r3 = lambda a: a.reshape(b, s, a.shape[-1])
        o_a = _mla_attn(r3(q_a), r3(k_a), r3(v_a))
        o_b = _dsa_attn(r3(q_b), r3(q_i), r3(w_i), r3(k_i), r3(k_b), r3(v_b), n_sel)
        lam_init = 0.8 - 0.6 * math.exp(-0.3 * i)
        consts = jnp.stack([pad_lanes(lam_q1[i]), pad_lanes(lam_k1[i]), pad_lanes(lam_q2[i]),
                            pad_lanes(lam_k2[i]), jnp.full((LANES,), lam_init, f32), subln_g[i],
                            jnp.zeros((LANES,), f32), jnp.zeros((LANES,), f32)]).astype(f32)
        o_c = _diff_attn(r3(q_c), r3(k_c), r3(v_c), consts)

        last = i == DEPTH - 1
        gn = (final_g if last else norm_g[i + 1]).reshape(1, d)
        res = _out_proj(o_a.reshape(n, -1), o_b.reshape(n, -1), o_c.reshape(n, -1), g, h,
                        p.reshape(DEPTH, n, PLE_DIM), w_o_c, w_pg_c, w_ple_c, i, gn, emit_h=not last)
        if last:
            out = res[0]
        else:
            h, u = res
    return out.reshape(b, s, d)
```

```python
import functools
import math

import jax
import jax.numpy as jnp
import numpy as np
from jax import lax
from jax.experimental import pallas as pl
from jax.experimental.pallas import tpu as pltpu

D_MODEL = 2048
DEPTH = 4
CHUNK = 64
ROPE_THETA = 500000.0
NORM_EPS = 1e-6
NEG_INF = -1e30
PLE_DIM = 256
HEAD_DIM = 128

A_HEADS = 6
A_Q_RANK = 384
A_KV_RANK = 256
A_NOPE = 128
A_ROPE = 64
A_V = 128
A_WIDTH = A_HEADS * A_V
B_HEADS = 5
B_WIDTH = B_HEADS * HEAD_DIM
IDX_HEADS = 8
IDX_DIM = 64
TOPK_MAX = 256
C_HEADS = 5
C_QK = 64
C_V = 128
C_WIDTH = C_HEADS * C_V
D_MIX = A_WIDTH + B_WIDTH + C_WIDTH

IN_NAMES = ("c_q", "c_kv", "k_rope", "g_a", "q_b", "k_b", "v_b", "q_idx", "k_idx", "w_idx", "g_b",
            "q_c", "k_c", "v_c", "g_c")
IN_SPLITS = (A_Q_RANK, A_KV_RANK, A_ROPE, A_WIDTH,
             B_WIDTH, B_WIDTH, B_WIDTH, IDX_HEADS * IDX_DIM, IDX_DIM, IDX_HEADS, B_WIDTH,
             C_HEADS * 2 * C_QK, C_HEADS * 2 * C_QK, C_WIDTH, C_WIDTH)

LANES = 128
A_QK_PAD = 256
INT_MIN = -2 ** 31
LOG2E = math.log2(math.e)
MXU_DTYPE = jnp.bfloat16
VMEM_LIMIT = 56 * 1024 * 1024

PROJ_ROWS = 512
OUT_ROWS = 256
ATTN_TILE = 512
ROPE_ROWS = 1024
RELAYOUT_COLS = 256

ROPE_PATTERNS = ((A_ROPE, 64), (HEAD_DIM // 4, 128), (IDX_DIM // 4, 64))
PAT_MLA, PAT_DSA, PAT_SMALL = 0, 1, 2

_NT = (((1,), (1,)), ((), ()))


def _params(n_axes):
    return pltpu.CompilerParams(dimension_semantics=("arbitrary",) * n_axes,
                                vmem_limit_bytes=VMEM_LIMIT)


def _dot(a, b):
    return jnp.dot(a, b, preferred_element_type=jnp.float32)


def _dot_nt(a, b):
    return lax.dot_general(a, b, _NT, preferred_element_type=jnp.float32)


def _sigmoid(x):
    return 1.0 / (1.0 + jnp.exp(-x))


def _rms(x, g):
    return x * lax.rsqrt(jnp.mean(x * x, axis=-1, keepdims=True) + NORM_EPS) * g


def _rope_pattern_rows():
    freq = []
    m_up = np.zeros((3, LANES), np.float32)
    m_dn = np.zeros((3, LANES), np.float32)
    for p, (n_rot, period) in enumerate(ROPE_PATTERNS):
        half = n_rot // 2
        inv = 1.0 / (ROPE_THETA ** (jnp.arange(half, dtype=jnp.float32) * (2.0 / n_rot)))
        one = jnp.concatenate([inv, inv, jnp.zeros((period - n_rot,), jnp.float32)])
        freq.append(jnp.tile(one, LANES // period))
        for lane in range(LANES):
            m = lane % period
            m_up[p, lane] = float(m < half)
            m_dn[p, lane] = float(half <= m < n_rot)
    return jnp.stack(freq), jnp.asarray(m_up), jnp.asarray(m_dn)


def _rope_tab_kernel(pos_ref, f_ref, mu_ref, md_ref, c_ref, sa_ref, sb_ref):
    pos = pos_ref[...]
    for p in range(3):
        ang = pos * f_ref[p:p + 1, :]
        sin = jnp.sin(ang)
        c_ref[p] = jnp.cos(ang)
        sa_ref[p] = -sin * mu_ref[p:p + 1, :]
        sb_ref[p] = sin * md_ref[p:p + 1, :]


def _rope_tables(pos_f32):
    n = pos_f32.shape[0]
    tm = min(n, ROPE_ROWS)
    freq, m_up, m_dn = _rope_pattern_rows()
    row = pl.BlockSpec((3, LANES), lambda i: (0, 0))
    tab = pl.BlockSpec((3, tm, LANES), lambda i: (0, i, 0))
    shape = jax.ShapeDtypeStruct((3, n, LANES), jnp.float32)
    return pl.pallas_call(
        _rope_tab_kernel,
        grid=(n // tm,),
        in_specs=[pl.BlockSpec((tm, 1), lambda i: (i, 0)), row, row, row],
        out_specs=[tab, tab, tab],
        out_shape=[shape, shape, shape],
        compiler_params=_params(1),
        name="rope_tables",
    )(pos_f32, freq, m_up, m_dn)


def _rope128(x, c, sa, sb, half):
    return x * c + pltpu.roll(x, LANES - half, 1) * sa + pltpu.roll(x, half, 1) * sb


def _norm_kernel(x_ref, g_ref, o_ref):
    o_ref[...] = _rms(x_ref[...], g_ref[...]).astype(o_ref.dtype)


def _norm(x, g, out_dtype):
    n, d = x.shape
    tm = min(n, PROJ_ROWS)
    return pl.pallas_call(
        _norm_kernel,
        grid=(n // tm,),
        in_specs=[pl.BlockSpec((tm, d), lambda i: (i, 0)), pl.BlockSpec((1, d), lambda i: (0, 0))],
        out_specs=pl.BlockSpec((tm, d), lambda i: (i, 0)),
        out_shape=jax.ShapeDtypeStruct((n, d), out_dtype),
        compiler_params=_params(1),
        name="rms_norm",
    )(x, g.reshape(1, d))


def _proj_kernel(*refs, segs, with_tables):
    u_ref, w_ref = refs[0], refs[1]
    if with_tables:
        c_ref, sa_ref, sb_ref = refs[2:5]
        outs = refs[5:]
    else:
        outs = refs[2:]
    u = u_ref[...]
    for (c0, width, pat, scale), o_ref in zip(segs, outs):
        z = _dot_nt(u, w_ref[c0:c0 + width, :])
        if pat is None and scale is None:
            o_ref[...] = z.astype(o_ref.dtype)
            continue
        for ch in range(width // LANES):
            sl = slice(ch * LANES, (ch + 1) * LANES)
            x = z[:, sl]
            if pat is not None:
                x = _rope128(x, c_ref[pat], sa_ref[pat], sb_ref[pat], ROPE_PATTERNS[pat][0] // 2)
            if scale is not None:
                x = x * scale
            o_ref[:, sl] = x.astype(o_ref.dtype)


def _layer_spec(stacked, layer, **kwargs):
    zeros = (0,) * (stacked.ndim - 1)
    return pl.BlockSpec((None,) + stacked.shape[1:], lambda i: (layer,) + zeros, **kwargs)


def _proj(u, w, layer, tabs, segs, out_dtypes, name):
    n, d = u.shape
    tm = min(n, PROJ_ROWS)
    in_specs = [pl.BlockSpec((tm, d), lambda i: (i, 0)), _layer_spec(w, layer)]
    args = [u, w]
    if tabs is not None:
        in_specs += [pl.BlockSpec((3, tm, LANES), lambda i: (0, i, 0))] * 3
        args += list(tabs)
    out_specs = [pl.BlockSpec((tm, s[1]), lambda i: (i, 0)) for s in segs]
    out_shape = [jax.ShapeDtypeStruct((n, s[1]), dt) for s, dt in zip(segs, out_dtypes)]
    return pl.pallas_call(
        functools.partial(_proj_kernel, segs=tuple(segs), with_tables=tabs is not None),
        grid=(n // tm,),
        in_specs=in_specs,
        out_specs=out_specs,
        out_shape=out_shape,
        compiler_params=_params(1),
        name=name,
    )(*args)


def _mla_proj_kernel(u_ref, w_ref, qg_ref, kvg_ref, wuq_ref, wukv_ref, c_ref, sa_ref, sb_ref,
                     q_ref, k_ref, v_ref):
    z = _dot_nt(u_ref[...], w_ref[...])
    c, sa, sb = c_ref[PAT_MLA], sa_ref[PAT_MLA], sb_ref[PAT_MLA]
    half = A_ROPE // 2
    scale = (A_NOPE + A_ROPE) ** -0.5 * LOG2E
    c_q = z[:, :A_Q_RANK]
    c_kv = z[:, A_Q_RANK:A_Q_RANK + A_KV_RANK]
    k_rope = z[:, A_Q_RANK + A_KV_RANK:]
    q = _dot(_rms(c_q, qg_ref[...]).astype(MXU_DTYPE), wuq_ref[...])
    kv = _dot(_rms(c_kv, kvg_ref[...]).astype(MXU_DTYPE), wukv_ref[...])
    k_pe = _rope128(k_rope, c, sa, sb, half).astype(k_ref.dtype)
    for h in range(A_HEADS):
        o = h * A_QK_PAD
        q_ref[:, o:o + A_NOPE] = (q[:, o:o + A_NOPE] * scale).astype(q_ref.dtype)
        q_rot = _rope128(q[:, o + A_NOPE:o + A_QK_PAD], c, sa, sb, half)
        q_ref[:, o + A_NOPE:o + A_QK_PAD] = (q_rot * scale).astype(q_ref.dtype)
        k_ref[:, o:o + A_NOPE] = kv[:, h * A_NOPE:(h + 1) * A_NOPE].astype(k_ref.dtype)
        k_ref[:, o + A_NOPE:o + A_QK_PAD] = k_pe
    v_ref[...] = kv[:, A_WIDTH:].astype(v_ref.dtype)


def _mla_proj(u, w, layer, qg, kvg, wuq, wukv, tabs):
    n, d = u.shape
    tm = min(n, PROJ_ROWS)
    full = lambda a: pl.BlockSpec(a.shape, lambda i: (0,) * a.ndim)
    tab = pl.BlockSpec((3, tm, LANES), lambda i: (0, i, 0))
    qk = A_HEADS * A_QK_PAD
    return pl.pallas_call(
        _mla_proj_kernel,
        grid=(n // tm,),
        in_specs=[pl.BlockSpec((tm, d), lambda i: (i, 0)), _layer_spec(w, layer), full(qg), full(kvg),
                  _layer_spec(wuq, layer), _layer_spec(wukv, layer), tab, tab, tab],
        out_specs=[pl.BlockSpec((tm, qk), lambda i: (i, 0)), pl.BlockSpec((tm, qk), lambda i: (i, 0)),
                   pl.BlockSpec((tm, A_WIDTH), lambda i: (i, 0))],
        out_shape=[jax.ShapeDtypeStruct((n, qk), MXU_DTYPE), jax.ShapeDtypeStruct((n, qk), MXU_DTYPE),
                   jax.ShapeDtypeStruct((n, A_WIDTH), MXU_DTYPE)],
        compiler_params=_params(1),
        name="mla_proj",
    )(u, w, qg, kvg, wuq, wukv, *tabs)


def _diag_bias(t):
    chunk = np.arange(t) // CHUNK
    return jnp.asarray(np.where(chunk[None, :] <= chunk[:, None], 0.0, NEG_INF), jnp.float32)


def _online_softmax_step(s, v, m_ref, acc_ref, first=False):
    chunks = [s[:, c * LANES:(c + 1) * LANES] for c in range(s.shape[1] // LANES)]
    m_new = jnp.max(functools.reduce(jnp.maximum, chunks), axis=-1, keepdims=True)
    if first:
        m_new = jnp.broadcast_to(m_new, m_ref.shape)
    else:
        m_prev = m_ref[...]
        m_new = jnp.maximum(m_prev, m_new)
        alpha = jnp.exp2(m_prev - m_new)
    p = jnp.concatenate([jnp.exp2(c - m_new).astype(v.dtype) for c in chunks], axis=1)
    pv = _dot(p, jnp.concatenate([v, jnp.ones_like(v)], axis=1))
    acc_ref[...] = pv if first else jnp.concatenate([alpha, alpha], axis=1) * acc_ref[...] + pv
    m_ref[...] = m_new


def _softmax_finish(acc_ref):
    acc = acc_ref[...]
    return acc[:, :LANES] / acc[:, LANES:]


MLA_HEADS_PER_STEP = 6


def _mla_attn_kernel(q_ref, k_ref, v_ref, bias_ref, o_ref, m_sc, acc_sc, *, tq):
    i = pl.program_id(2)

    def tile(j, diagonal):
        rows = pl.ds(pl.multiple_of(j * tq, tq), tq)
        for h in range(MLA_HEADS_PER_STEP):
            qk = slice(h * A_QK_PAD, (h + 1) * A_QK_PAD)
            s = _dot_nt(q_ref[:, qk], k_ref[rows, qk])
            if diagonal:
                s = s + bias_ref[...]
            _online_softmax_step(s, v_ref[rows, h * A_V:(h + 1) * A_V], m_sc.at[h], acc_sc.at[h], first=diagonal)

    def body(j, carry):
        tile(j, False)
        return carry

    tile(i, True)
    lax.fori_loop(0, i, body, 0)
    for h in range(MLA_HEADS_PER_STEP):
        o_ref[:, h * A_V:(h + 1) * A_V] = _softmax_finish(acc_sc.at[h])


def _mla_attn(q, k, v):
    b, s, _ = q.shape
    tq = min(s, ATTN_TILE)
    hp = MLA_HEADS_PER_STEP
    stat = pltpu.VMEM((hp, tq, LANES), jnp.float32)
    return pl.pallas_call(
        functools.partial(_mla_attn_kernel, tq=tq),
        grid=(b, A_HEADS // hp, s // tq),
        in_specs=[pl.BlockSpec((None, tq, hp * A_QK_PAD), lambda bi, h, i: (bi, i, h)),
                  pl.BlockSpec((None, s, hp * A_QK_PAD), lambda bi, h, i: (bi, 0, h),
                               pipeline_mode=pl.Buffered(1)),
                  pl.BlockSpec((None, s, hp * A_V), lambda bi, h, i: (bi, 0, h),
                               pipeline_mode=pl.Buffered(1)),
                  pl.BlockSpec((tq, tq), lambda bi, h, i: (0, 0))],
        out_specs=pl.BlockSpec((None, tq, hp * A_V), lambda bi, h, i: (bi, i, h)),
        out_shape=jax.ShapeDtypeStruct((b, s, A_WIDTH), jnp.float32),
        scratch_shapes=[stat, pltpu.VMEM((hp, tq, 2 * LANES), jnp.float32)],
        compiler_params=_params(3),
        name="mla_attn",
    )(q, k, v, _diag_bias(tq))


def _diff_attn_kernel(q_ref, k_ref, v_ref, c_ref, bias_ref, o_ref, qs_sc, m_sc, acc_sc, *, tq):
    i = pl.program_id(1)
    lane = lax.broadcasted_iota(jnp.int32, (tq, 2 * C_QK), 1)
    for h in range(C_HEADS):
        q = q_ref[:, h * 2 * C_QK:(h + 1) * 2 * C_QK]
        qs_sc[2 * h] = jnp.where(lane < C_QK, q, jnp.zeros_like(q))
        qs_sc[2 * h + 1] = jnp.where(lane >= C_QK, q, jnp.zeros_like(q))

    def tile(j, diagonal):
        rows = pl.ds(pl.multiple_of(j * tq, tq), tq)
        for h in range(C_HEADS):
            k = k_ref[rows, h * 2 * C_QK:(h + 1) * 2 * C_QK]
            v = v_ref[rows, h * C_V:(h + 1) * C_V]
            for t in range(2):
                n = 2 * h + t
                s = _dot_nt(qs_sc[n], k)
                if diagonal:
                    s = s + bias_ref[...]
                _online_softmax_step(s, v, m_sc.at[n], acc_sc.at[n], first=diagonal)

    def body(j, carry):
        tile(j, False)
        return carry

    tile(i, True)
    lax.fori_loop(0, i, body, 0)

    c = c_ref[...]
    lam_init = c[4:5, 0:1]
    lam = (jnp.exp(jnp.sum(c[0:1] * c[1:2], axis=-1, keepdims=True))
           - jnp.exp(jnp.sum(c[2:3] * c[3:4], axis=-1, keepdims=True)) + lam_init)
    for h in range(C_HEADS):
        o = _softmax_finish(acc_sc.at[2 * h]) - lam * _softmax_finish(acc_sc.at[2 * h + 1])
        o_ref[:, h * C_V:(h + 1) * C_V] = _rms(o, c[5:6]) * (1.0 - lam_init)


def _diff_attn(q, k, v, consts):
    b, s, _ = q.shape
    tq = min(s, ATTN_TILE)
    n_chain = 2 * C_HEADS
    stat = pltpu.VMEM((n_chain, tq, LANES), jnp.float32)
    resident = lambda width: pl.BlockSpec((None, s, width), lambda bi, i: (bi, 0, 0))
    return pl.pallas_call(
        functools.partial(_diff_attn_kernel, tq=tq),
        grid=(b, s // tq),
        in_specs=[pl.BlockSpec((None, tq, C_WIDTH), lambda bi, i: (bi, i, 0)),
                  resident(C_WIDTH), resident(C_WIDTH),
                  pl.BlockSpec((8, LANES), lambda bi, i: (0, 0)),
                  pl.BlockSpec((tq, tq), lambda bi, i: (0, 0))],
        out_specs=pl.BlockSpec((None, tq, C_WIDTH), lambda bi, i: (bi, i, 0)),
        out_shape=jax.ShapeDtypeStruct((b, s, C_WIDTH), jnp.float32),
        scratch_shapes=[pltpu.VMEM((n_chain, tq, 2 * C_QK), MXU_DTYPE), stat,
                        pltpu.VMEM((n_chain, tq, 2 * LANES), jnp.float32)],
        compiler_params=_params(2),
        name="diff_attn",
    )(q, k, v, consts, _diag_bias(tq))


SUBLANES = 8
COUNT_CHAINS = 4
PACKED_ROWS = 16
HALF_BITS = 16
HALF_MIN = -2 ** (HALF_BITS - 1)
IDX_GROUP = 4


def _tree_sum(parts):
    while len(parts) > 1:
        parts = [a + b for a, b in zip(parts[0::2], parts[1::2])] + parts[len(parts) & ~1:]
    return parts[0]


def _dsa_kernel(qb_ref, qi_ref, w_ref, ki_ref, kb_ref, vb_ref, o_ref,
                keys_sc, half_sc, qz_sc, m_sc, acc_sc, *, tq, n_sel):
    i = pl.program_id(1)
    tk = tq
    n_tiles = i + 1
    f32 = jnp.float32

    lane = lax.broadcasted_iota(jnp.int32, (tq, LANES), 1)
    for h in range(IDX_HEADS):
        pair = qi_ref[:, (h // 2) * LANES:(h // 2 + 1) * LANES]
        keep = (lane < IDX_DIM) if h % 2 == 0 else (lane >= IDX_DIM)
        qz_sc[h * tq:(h + 1) * tq, :] = jnp.where(keep, pair, jnp.zeros_like(pair))
    w_t = w_ref[...].T

    def score_tile(j, masked):
        rows = pl.ds(pl.multiple_of(j * tk, tk), tk)
        ki = ki_ref[rows, :]
        sc = jnp.zeros((tk, tq), f32)
        for h0 in range(0, IDX_HEADS, IDX_GROUP):
            rel = _dot_nt(ki, qz_sc[h0 * tq:(h0 + IDX_GROUP) * tq, :])
            for g in range(IDX_GROUP):
                sc = sc + jnp.maximum(rel[:, g * tq:(g + 1) * tq], 0.0) * w_t[h0 + g:h0 + g + 1, :]
        if masked:
            kc = (j * tk + lax.broadcasted_iota(jnp.int32, (tk, tq), 0)) // CHUNK
            qc = (i * tq + lax.broadcasted_iota(jnp.int32, (tk, tq), 1)) // CHUNK
            sc = jnp.where(kc <= qc, sc, NEG_INF)
        bits = lax.bitcast_convert_type(sc, jnp.int32)
        key = jnp.where(bits < 0, INT_MIN - bits, bits)
        key = jnp.where(sc > 0.5 * NEG_INF, key, INT_MIN)
        keys_sc[j] = key
        half_sc[j] = (key >> HALF_BITS).astype(jnp.int16)

    def score_body(j, carry):
        score_tile(j, False)
        return carry

    lax.fori_loop(0, n_tiles - 1, score_body, 0)
    score_tile(n_tiles - 1, True)

    def count_ge(t):
        def body(j, accs):
            accs = list(accs)
            for n, r in enumerate(range(0, tk, SUBLANES)):
                hit = jnp.where(keys_sc[j, r:r + SUBLANES, :] >= t, 1.0, 0.0)
                accs[n % COUNT_CHAINS] = accs[n % COUNT_CHAINS] + hit
            return tuple(accs)
        zero = jnp.zeros((SUBLANES, tq), f32)
        accs = lax.fori_loop(0, n_tiles, body, (zero,) * COUNT_CHAINS)
        return jnp.sum(_tree_sum(list(accs)), axis=0, keepdims=True)

    def count16(ref, t, strict):
        t16 = t.astype(jnp.int16)
        one, zero = jnp.ones((), jnp.int16), jnp.zeros((), jnp.int16)

        def body(j, accs):
            accs = list(accs)
            for n, r in enumerate(range(0, tk, PACKED_ROWS)):
                x = ref[j, r:r + PACKED_ROWS, :]
                hit = jnp.where((x > t16) if strict else (x >= t16), one, zero)
                accs[n % COUNT_CHAINS] = accs[n % COUNT_CHAINS] + hit
            return tuple(accs)

        accs = lax.fori_loop(0, n_tiles, body, (jnp.zeros((PACKED_ROWS, tq), jnp.int16),) * COUNT_CHAINS)
        return jnp.sum(_tree_sum(list(accs)).astype(jnp.int32), axis=0, keepdims=True)

    def kth_largest16(ref, k_need):
        def step(it, t):
            cand = t + lax.shift_left(jnp.int32(1), jnp.int32(HALF_BITS - 1) - it)
            return jnp.where(count16(ref, cand, False) >= k_need, cand, t)
        return lax.fori_loop(0, HALF_BITS, step, jnp.full((1, tq), HALF_MIN, jnp.int32))

    thr_hi = kth_largest16(half_sc, n_sel)
    k_lo = n_sel - count16(half_sc, thr_hi, True)
    thr_hi16 = thr_hi.astype(jnp.int16)

    def lo_body(j, carry):
        lo = ((keys_sc[j] & (2 ** HALF_BITS - 1)) + HALF_MIN).astype(jnp.int16)
        half_sc[j] = jnp.where(half_sc[j] == thr_hi16, lo, jnp.int16(HALF_MIN))
        return carry

    lax.fori_loop(0, n_tiles, lo_body, 0)
    thr_lo = kth_largest16(half_sc, k_lo)
    thr = thr_hi * 2 ** HALF_BITS + (thr_lo - HALF_MIN)
    thr = jnp.maximum(thr, INT_MIN + 1)
    k_f = float(n_sel)
    cnt_ge = count_ge(thr)

    @pl.when(jnp.max(cnt_ge) > k_f)
    def _():
        keep_n = k_f - count_ge(thr + 1)
        tri = (lax.broadcasted_iota(jnp.int32, (tk, tk), 0)
               >= lax.broadcasted_iota(jnp.int32, (tk, tk), 1)).astype(MXU_DTYPE)

        def body(j, carry):
            kt = keys_sc[j]
            eq = kt == thr
            rank = carry + _dot(tri, jnp.where(eq, 1.0, 0.0).astype(MXU_DTYPE))
            keys_sc[j] = jnp.where(eq & (rank > keep_n), INT_MIN, kt)
            return rank[tk - 1:tk, :]

        lax.fori_loop(0, n_tiles, body, jnp.zeros((1, tq), f32))

    def bias_body(j, carry):
        bias = jnp.where(keys_sc[j] >= thr, 0.0, NEG_INF).T
        keys_sc[j] = lax.bitcast_convert_type(bias, jnp.int32)
        return carry

    lax.fori_loop(0, n_tiles, bias_body, 0)

    def attn_tile(j, first):
        rows = pl.ds(j * tk if first else pl.multiple_of(j * tk, tk), tk)
        bias = lax.bitcast_convert_type(keys_sc[j], f32)
        for h in range(B_HEADS):
            sl = slice(h * HEAD_DIM, (h + 1) * HEAD_DIM)
            s = _dot_nt(qb_ref[:, sl], kb_ref[rows, sl]) + bias
            _online_softmax_step(s, vb_ref[rows, sl], m_sc.at[h], acc_sc.at[h], first=first)

    def attn_body(j, carry):
        attn_tile(j, False)
        return carry

    attn_tile(0, True)
    lax.fori_loop(1, n_tiles, attn_body, 0)
    for h in range(B_HEADS):
        o_ref[:, h * HEAD_DIM:(h + 1) * HEAD_DIM] = _softmax_finish(acc_sc.at[h])


def _dsa_attn(qb, qi, w, ki, kb, vb, n_sel):
    b, s, _ = qb.shape
    tq = min(s, ATTN_TILE)
    blk = lambda width: pl.BlockSpec((None, tq, width), lambda bi, i: (bi, i, 0))
    res = lambda width: pl.BlockSpec((None, s, width), lambda bi, i: (bi, 0, 0), pipeline_mode=pl.Buffered(1))
    return pl.pallas_call(
        functools.partial(_dsa_kernel, tq=tq, n_sel=n_sel),
        grid=(b, s // tq),
        in_specs=[blk(B_WIDTH), blk(IDX_HEADS * IDX_DIM), blk(LANES), res(LANES), res(B_WIDTH), res(B_WIDTH)],
        out_specs=blk(B_WIDTH),
        out_shape=jax.ShapeDtypeStruct((b, s, B_WIDTH), jnp.float32),
        scratch_shapes=[pltpu.VMEM((s // tq, tq, tq), jnp.int32),
                        pltpu.VMEM((s // tq, tq, tq), jnp.int16),
                        pltpu.VMEM((IDX_HEADS * tq, LANES), MXU_DTYPE),
                        pltpu.VMEM((B_HEADS, tq, LANES), jnp.float32),
                        pltpu.VMEM((B_HEADS, tq, 2 * LANES), jnp.float32)],
        compiler_params=_params(2),
        name="dsa_attn",
    )(qb, qi, w, ki, kb, vb)


def _out_kernel(oa_ref, ob_ref, oc_ref, g_ref, h_ref, p_ref, wo_ref, wpg_ref, wple_ref, gn_ref,
                *out_refs, emit_h):
    g = g_ref[...]
    sg = g * _sigmoid(g)
    h1 = h_ref[...]
    col = 0
    for o_ref in (oa_ref, ob_ref, oc_ref):
        width = o_ref.shape[1]
        mixed = (o_ref[...] * sg[:, col:col + width]).astype(MXU_DTYPE)
        h1 = h1 + _dot(mixed, wo_ref[col:col + width, :])
        col += width
    gate = _sigmoid(_dot(h1.astype(MXU_DTYPE), wpg_ref[...]))
    h2 = h1 + _dot(p_ref[...].astype(MXU_DTYPE), wple_ref[...]) * gate
    if emit_h:
        out_refs[0][...] = h2
    out_refs[-1][...] = _rms(h2, gn_ref[...]).astype(out_refs[-1].dtype)


def _out_proj(oa, ob, oc, g, h, p, wo, wpg, wple, layer, gn, emit_h):
    n, d = h.shape
    tm = min(n, OUT_ROWS)
    row = lambda a: pl.BlockSpec((tm, a.shape[1]), lambda i: (i, 0))
    const = lambda a: _layer_spec(a, layer, pipeline_mode=pl.Buffered(1))
    out_specs = [pl.BlockSpec((tm, d), lambda i: (i, 0))]
    out_shape = [jax.ShapeDtypeStruct((n, d), MXU_DTYPE if emit_h else jnp.float32)]
    if emit_h:
        out_specs = [pl.BlockSpec((tm, d), lambda i: (i, 0))] + out_specs
        out_shape = [jax.ShapeDtypeStruct((n, d), jnp.float32)] + out_shape
    return pl.pallas_call(
        functools.partial(_out_kernel, emit_h=emit_h),
        grid=(n // tm,),
        in_specs=[row(oa), row(ob), row(oc), row(g), row(h),
                  pl.BlockSpec((None, tm, p.shape[2]), lambda i: (layer, i, 0)),
                  const(wo), const(wpg), const(wple), pl.BlockSpec((1, d), lambda i: (0, 0))],
        out_specs=out_specs,
        out_shape=out_shape,
        compiler_params=_params(1),
        name="out_proj",
    )(oa, ob, oc, g, h, p, wo, wpg, wple, gn)


def _in_cols(name):
    k = IN_NAMES.index(name)
    start = sum(IN_SPLITS[:k])
    return start, start + IN_SPLITS[k]


_W_BLOCKS = (("c_q", "c_kv", "k_rope", LANES - A_ROPE),
             ("g_a", "g_b", "g_c"),
             ("q_b", "k_b", "v_b", "q_idx", "k_idx", "k_idx", "w_idx", LANES - IDX_HEADS),
             ("q_c", "k_c", "v_c"))


def _block_width(parts):
    return sum(p if isinstance(p, int) else IN_SPLITS[IN_NAMES.index(p)] for p in parts)


def _relayout_kernel(wt_ref, *out_refs):
    for parts, o_ref in zip(_W_BLOCKS, out_refs):
        row, pieces = 0, []
        for part in parts:
            if isinstance(part, int):
                pieces.append(jnp.zeros((part, wt_ref.shape[1]), jnp.float32))
            else:
                a, b = _in_cols(part)
                pieces.append(wt_ref[a:b, :])
            rows = sum(x.shape[0] for x in pieces)
            if rows % PACKED_ROWS == 0:
                o_ref[row:row + rows, :] = jnp.concatenate(pieces, axis=0).astype(o_ref.dtype)
                row, pieces = row + rows, []
        assert not pieces


def _relayout_w_in(w_in):
    depth, d, d_in = w_in.shape
    wt = jnp.transpose(w_in, (0, 2, 1))
    tc = RELAYOUT_COLS
    widths = [_block_width(parts) for parts in _W_BLOCKS]
    return pl.pallas_call(
        _relayout_kernel,
        grid=(depth, d // tc),
        in_specs=[pl.BlockSpec((None, d_in, tc), lambda l, c: (l, 0, c))],
        out_specs=[pl.BlockSpec((None, w, tc), lambda l, c: (l, 0, c)) for w in widths],
        out_shape=[jax.ShapeDtypeStruct((depth, w, d), MXU_DTYPE) for w in widths],
        compiler_params=_params(2),
        name="relayout_w_in",
    )(wt)


_SEG_G = ((0, D_MIX, None, None),)
_SEG_B = ((0, B_WIDTH, PAT_DSA, HEAD_DIM ** -0.5 * LOG2E), (B_WIDTH, B_WIDTH, PAT_DSA, None),
          (2 * B_WIDTH, B_WIDTH, None, None), (3 * B_WIDTH, IDX_HEADS * IDX_DIM, PAT_SMALL, None),
          (3 * B_WIDTH + IDX_HEADS * IDX_DIM, LANES, PAT_SMALL, None),
          (3 * B_WIDTH + IDX_HEADS * IDX_DIM + LANES, LANES, None, None))
_SEG_C = ((0, C_WIDTH, PAT_SMALL, C_QK ** -0.5 * LOG2E), (C_WIDTH, C_WIDTH, PAT_SMALL, None),
          (2 * C_WIDTH, C_WIDTH, None, None))


def kernel(x, p, positions, w_in, w_uq, w_ukv, w_o, norm_g, q_norm_g, kv_norm_g,
           lam_q1, lam_k1, lam_q2, lam_k2, subln_g, w_ple, w_pg, final_g):
    b, s, d = x.shape
    n = b * s
    f32 = jnp.float32
    n_sel = min(TOPK_MAX, s // 4)

    w_a, w_g, w_b, w_c = _relayout_w_in(w_in)
    uq = w_uq.reshape(DEPTH, A_Q_RANK, A_HEADS, A_NOPE + A_ROPE)
    uq = jnp.pad(uq, ((0, 0), (0, 0), (0, 0), (0, A_QK_PAD - A_NOPE - A_ROPE)))
    uq = uq.reshape(DEPTH, A_Q_RANK, A_HEADS * A_QK_PAD).astype(MXU_DTYPE)
    ukv = w_ukv.reshape(DEPTH, A_KV_RANK, A_HEADS, A_NOPE + A_V)
    ukv = jnp.concatenate([ukv[..., :A_NOPE].reshape(DEPTH, A_KV_RANK, A_WIDTH),
                           ukv[..., A_NOPE:].reshape(DEPTH, A_KV_RANK, A_WIDTH)], axis=-1).astype(MXU_DTYPE)
    w_o_c = w_o.astype(MXU_DTYPE)
    w_pg_c = w_pg.astype(MXU_DTYPE)
    w_ple_c = w_ple.astype(MXU_DTYPE)
    pad_lanes = lambda v: jnp.pad(v, (0, LANES - v.shape[0]))

    tabs = _rope_tables(positions.astype(f32).reshape(n, 1))
    h = x.reshape(n, d)
    u = _norm(h, norm_g[0], MXU_DTYPE)
    out = None
    for i in range(DEPTH):
        q_a, k_a, v_a = _mla_proj(u, w_a, i, q_norm_g[i].reshape(1, -1), kv_norm_g[i].reshape(1, -1),
                                  uq, ukv, tabs)
        (g,) = _proj(u, w_g, i, None, _SEG_G, (f32,), "proj_gates")
        q_b, k_b, v_b, q_i, k_i, w_i = _proj(u, w_b, i, tabs, _SEG_B,
                                             (MXU_DTYPE,) * 5 + (f32,), "proj_dsa")
        q_c, k_c, v_c = _proj(u, w_c, i, tabs, _SEG_C, (MXU_DTYPE,) * 3, "proj_diff")

        r3 = lambda a: a.reshape(b, s, a.shape[-1])
        o_a = _mla_attn(r3(q_a), r3(k_a), r3(v_a))
        o_b = _dsa_attn(r3(q_b), r3(q_i), r3(w_i), r3(k_i), r3(k_b), r3(v_b), n_sel)
        lam_init = 0.8 - 0.6 * math.exp(-0.3 * i)
        consts = jnp.stack([pad_lanes(lam_q1[i]), pad_lanes(lam_k1[i]), pad_lanes(lam_q2[i]),
                            pad_lanes(lam_k2[i]), jnp.full((LANES,), lam_init, f32), subln_g[i],
                            jnp.zeros((LANES,), f32), jnp.zeros((LANES,), f32)]).astype(f32)
        o_c = _diff_attn(r3(q_c), r3(k_c), r3(v_c), consts)

        last = i == DEPTH - 1
        gn = (final_g if last else norm_g[i + 1]).reshape(1, d)
        res = _out_proj(o_a.reshape(n, -1), o_b.reshape(n, -1), o_c.reshape(n, -1), g, h,
                        p.reshape(DEPTH, n, PLE_DIM), w_o_c, w_pg_c, w_ple_c, i, gn, emit_h=not last)
        if last:
            out = res[0]
        else:
            h, u = res
    return out.reshape(b, s, d)
```

```python
import functools
import math

import jax
import jax.numpy as jnp
import numpy as np
from jax import lax
from jax.experimental import pallas as pl
from jax.experimental.pallas import tpu as pltpu

D_MODEL = 2048
DEPTH = 4
CHUNK = 64
ROPE_THETA = 500000.0
NORM_EPS = 1e-6
NEG_INF = -1e30
PLE_DIM = 256
HEAD_DIM = 128

A_HEADS = 6
A_Q_RANK = 384
A_KV_RANK = 256
A_NOPE = 128
A_ROPE = 64
A_V = 128
A_WIDTH = A_HEADS * A_V
B_HEADS = 5
B_WIDTH = B_HEADS * HEAD_DIM
IDX_HEADS = 8
IDX_DIM = 64
TOPK_MAX = 256
C_HEADS = 5
C_QK = 64
C_V = 128
C_WIDTH = C_HEADS * C_V
D_MIX = A_WIDTH + B_WIDTH + C_WIDTH

IN_NAMES = ("c_q", "c_kv", "k_rope", "g_a", "q_b", "k_b", "v_b", "q_idx", "k_idx", "w_idx", "g_b",
            "q_c", "k_c", "v_c", "g_c")
IN_SPLITS = (A_Q_RANK, A_KV_RANK, A_ROPE, A_WIDTH,
             B_WIDTH, B_WIDTH, B_WIDTH, IDX_HEADS * IDX_DIM, IDX_DIM, IDX_HEADS, B_WIDTH,
             C_HEADS * 2 * C_QK, C_HEADS * 2 * C_QK, C_WIDTH, C_WIDTH)

LANES = 128
A_QK_PAD = 256
INT_MIN = -2 ** 31
LOG2E = math.log2(math.e)
MXU_DTYPE = jnp.bfloat16
VMEM_LIMIT = 56 * 1024 * 1024

PROJ_ROWS = 512
OUT_ROWS = 256
ATTN_TILE = 512
ROPE_ROWS = 1024
RELAYOUT_COLS = 256

ROPE_PATTERNS = ((A_ROPE, 64), (HEAD_DIM // 4, 128), (IDX_DIM // 4, 64))
PAT_MLA, PAT_DSA, PAT_SMALL = 0, 1, 2

_NT = (((1,), (1,)), ((), ()))


def _params(n_axes):
    return pltpu.CompilerParams(dimension_semantics=("arbitrary",) * n_axes,
                                vmem_limit_bytes=VMEM_LIMIT)


def _dot(a, b):
    return jnp.dot(a, b, preferred_element_type=jnp.float32)


def _dot_nt(a, b):
    return lax.dot_general(a, b, _NT, preferred_element_type=jnp.float32)


def _sigmoid(x):
    return 1.0 / (1.0 + jnp.exp(-x))


def _rms(x, g):
    return x * lax.rsqrt(jnp.mean(x * x, axis=-1, keepdims=True) + NORM_EPS) * g


def _rope_pattern_rows():
    freq = []
    m_up = np.zeros((3, LANES), np.float32)
    m_dn = np.zeros((3, LANES), np.float32)
    for p, (n_rot, period) in enumerate(ROPE_PATTERNS):
        half = n_rot // 2
        inv = 1.0 / (ROPE_THETA ** (jnp.arange(half, dtype=jnp.float32) * (2.0 / n_rot)))
        one = jnp.concatenate([inv, inv, jnp.zeros((period - n_rot,), jnp.float32)])
        freq.append(jnp.tile(one, LANES // period))
        for lane in range(LANES):
            m = lane % period
            m_up[p, lane] = float(m < half)
            m_dn[p, lane] = float(half <= m < n_rot)
    return jnp.stack(freq), jnp.asarray(m_up), jnp.asarray(m_dn)


def _rope_tab_kernel(pos_ref, f_ref, mu_ref, md_ref, c_ref, sa_ref, sb_ref):
    pos = pos_ref[...]
    for p in range(3):
        ang = pos * f_ref[p:p + 1, :]
        sin = jnp.sin(ang)
        c_ref[p] = jnp.cos(ang)
        sa_ref[p] = -sin * mu_ref[p:p + 1, :]
        sb_ref[p] = sin * md_ref[p:p + 1, :]


def _rope_tables(pos_f32):
    n = pos_f32.shape[0]
    tm = min(n, ROPE_ROWS)
    freq, m_up, m_dn = _rope_pattern_rows()
    row = pl.BlockSpec((3, LANES), lambda i: (0, 0))
    tab = pl.BlockSpec((3, tm, LANES), lambda i: (0, i, 0))
    shape = jax.ShapeDtypeStruct((3, n, LANES), jnp.float32)
    return pl.pallas_call(
        _rope_tab_kernel,
        grid=(n // tm,),
        in_specs=[pl.BlockSpec((tm, 1), lambda i: (i, 0)), row, row, row],
        out_specs=[tab, tab, tab],
        out_shape=[shape, shape, shape],
        compiler_params=_params(1),
        name="rope_tables",
    )(pos_f32, freq, m_up, m_dn)


def _rope128(x, c, sa, sb, half):
    return x * c + pltpu.roll(x, LANES - half, 1) * sa + pltpu.roll(x, half, 1) * sb


def _norm_kernel(x_ref, g_ref, o_ref):
    o_ref[...] = _rms(x_ref[...], g_ref[...]).astype(o_ref.dtype)


def _norm(x, g, out_dtype):
    n, d = x.shape
    tm = min(n, PROJ_ROWS)
    return pl.pallas_call(
        _norm_kernel,
        grid=(n // tm,),
        in_specs=[pl.BlockSpec((tm, d), lambda i: (i, 0)), pl.BlockSpec((1, d), lambda i: (0, 0))],
        out_specs=pl.BlockSpec((tm, d), lambda i: (i, 0)),
        out_shape=jax.ShapeDtypeStruct((n, d), out_dtype),
        compiler_params=_params(1),
        name="rms_norm",
    )(x, g.reshape(1, d))


def _proj_kernel(*refs, segs, with_tables):
    u_ref, w_ref = refs[0], refs[1]
    if with_tables:
        c_ref, sa_ref, sb_ref = refs[2:5]
        outs = refs[5:]
    else:
        outs = refs[2:]
    u = u_ref[...]
    for (c0, width, pat, scale), o_ref in zip(segs, outs):
        z = _dot_nt(u, w_ref[c0:c0 + width, :])
        if pat is None and scale is None:
            o_ref[...] = z.astype(o_ref.dtype)
            continue
        for ch in range(width // LANES):
            sl = slice(ch * LANES, (ch + 1) * LANES)
            x = z[:, sl]
            if pat is not None:
                x = _rope128(x, c_ref[pat], sa_ref[pat], sb_ref[pat], ROPE_PATTERNS[pat][0] // 2)
            if scale is not None:
                x = x * scale
            o_ref[:, sl] = x.astype(o_ref.dtype)


def _layer_spec(stacked, layer, **kwargs):
    zeros = (0,) * (stacked.ndim - 1)
    return pl.BlockSpec((None,) + stacked.shape[1:], lambda i: (layer,) + zeros, **kwargs)


def _proj(u, w, layer, tabs, segs, out_dtypes, name):
    n, d = u.shape
    tm = min(n, PROJ_ROWS)
    in_specs = [pl.BlockSpec((tm, d), lambda i: (i, 0)), _layer_spec(w, layer)]
    args = [u, w]
    if tabs is not None:
        in_specs += [pl.BlockSpec((3, tm, LANES), lambda i: (0, i, 0))] * 3
        args += list(tabs)
    out_specs = [pl.BlockSpec((tm, s[1]), lambda i: (i, 0)) for s in segs]
    out_shape = [jax.ShapeDtypeStruct((n, s[1]), dt) for s, dt in zip(segs, out_dtypes)]
    return pl.pallas_call(
        functools.partial(_proj_kernel, segs=tuple(segs), with_tables=tabs is not None),
        grid=(n // tm,),
        in_specs=in_specs,
        out_specs=out_specs,
        out_shape=out_shape,
        compiler_params=_params(1),
        name=name,
    )(*args)


def _mla_proj_kernel(u_ref, w_ref, qg_ref, kvg_ref, wuq_ref, wukv_ref, c_ref, sa_ref, sb_ref,
                     q_ref, k_ref, v_ref):
    z = _dot_nt(u_ref[...], w_ref[...])
    c, sa, sb = c_ref[PAT_MLA], sa_ref[PAT_MLA], sb_ref[PAT_MLA]
    half = A_ROPE // 2
    scale = (A_NOPE + A_ROPE) ** -0.5 * LOG2E
    c_q = z[:, :A_Q_RANK]
    c_kv = z[:, A_Q_RANK:A_Q_RANK + A_KV_RANK]
    k_rope = z[:, A_Q_RANK + A_KV_RANK:]
    q = _dot(_rms(c_q, qg_ref[...]).astype(MXU_DTYPE), wuq_ref[...])
    kv = _dot(_rms(c_kv, kvg_ref[...]).astype(MXU_DTYPE), wukv_ref[...])
    k_pe = _rope128(k_rope, c, sa, sb, half).astype(k_ref.dtype)
    for h in range(A_HEADS):
        o = h * A_QK_PAD
        q_ref[:, o:o + A_NOPE] = (q[:, o:o + A_NOPE] * scale).astype(q_ref.dtype)
        q_rot = _rope128(q[:, o + A_NOPE:o + A_QK_PAD], c, sa, sb, half)
        q_ref[:, o + A_NOPE:o + A_QK_PAD] = (q_rot * scale).astype(q_ref.dtype)
        k_ref[:, o:o + A_NOPE] = kv[:, h * A_NOPE:(h + 1) * A_NOPE].astype(k_ref.dtype)
        k_ref[:, o + A_NOPE:o + A_QK_PAD] = k_pe
    v_ref[...] = kv[:, A_WIDTH:].astype(v_ref.dtype)


def _mla_proj(u, w, layer, qg, kvg, wuq, wukv, tabs):
    n, d = u.shape
    tm = min(n, PROJ_ROWS)
    full = lambda a: pl.BlockSpec(a.shape, lambda i: (0,) * a.ndim)
    tab = pl.BlockSpec((3, tm, LANES), lambda i: (0, i, 0))
    qk = A_HEADS * A_QK_PAD
    return pl.pallas_call(
        _mla_proj_kernel,
        grid=(n // tm,),
        in_specs=[pl.BlockSpec((tm, d), lambda i: (i, 0)), _layer_spec(w, layer), full(qg), full(kvg),
                  _layer_spec(wuq, layer), _layer_spec(wukv, layer), tab, tab, tab],
        out_specs=[pl.BlockSpec((tm, qk), lambda i: (i, 0)), pl.BlockSpec((tm, qk), lambda i: (i, 0)),
                   pl.BlockSpec((tm, A_WIDTH), lambda i: (i, 0))],
        out_shape=[jax.ShapeDtypeStruct((n, qk), MXU_DTYPE), jax.ShapeDtypeStruct((n, qk), MXU_DTYPE),
                   jax.ShapeDtypeStruct((n, A_WIDTH), MXU_DTYPE)],
        compiler_params=_params(1),
        name="mla_proj",
    )(u, w, qg, kvg, wuq, wukv, *tabs)


def _diag_bias(t):
    chunk = np.arange(t) // CHUNK
    return jnp.asarray(np.where(chunk[None, :] <= chunk[:, None], 0.0, NEG_INF), jnp.float32)


def _online_softmax_step(s, v, m_ref, acc_ref, first=False):
    chunks = [s[:, c * LANES:(c + 1) * LANES] for c in range(s.shape[1] // LANES)]
    m_new = jnp.max(functools.reduce(jnp.maximum, chunks), axis=-1, keepdims=True)
    if first:
        m_new = jnp.broadcast_to(m_new, m_ref.shape)
    else:
        m_prev = m_ref[...]
        m_new = jnp.maximum(m_prev, m_new)
        alpha = jnp.exp2(m_prev - m_new)
    p = jnp.concatenate([jnp.exp2(c - m_new).astype(v.dtype) for c in chunks], axis=1)
    pv = _dot(p, jnp.concatenate([v, jnp.ones_like(v)], axis=1))
    acc_ref[...] = pv if first else jnp.concatenate([alpha, alpha], axis=1) * acc_ref[...] + pv
    m_ref[...] = m_new


def _softmax_finish(acc_ref):
    acc = acc_ref[...]
    return acc[:, :LANES] / acc[:, LANES:]


MLA_HEADS_PER_STEP = 6


def _mla_attn_kernel(q_ref, k_ref, v_ref, bias_ref, o_ref, m_sc, acc_sc, *, tq):
    i = pl.program_id(2)

    def tile(j, diagonal):
        rows = pl.ds(pl.multiple_of(j * tq, tq), tq)
        for h in range(MLA_HEADS_PER_STEP):
            qk = slice(h * A_QK_PAD, (h + 1) * A_QK_PAD)
            s = _dot_nt(q_ref[:, qk], k_ref[rows, qk])
            if diagonal:
                s = s + bias_ref[...]
            _online_softmax_step(s, v_ref[rows, h * A_V:(h + 1) * A_V], m_sc.at[h], acc_sc.at[h], first=diagonal)

    def body(j, carry):
        tile(j, False)
        return carry

    tile(i, True)
    lax.fori_loop(0, i, body, 0)
    for h in range(MLA_HEADS_PER_STEP):
        o_ref[:, h * A_V:(h + 1) * A_V] = _softmax_finish(acc_sc.at[h])


def _mla_attn(q, k, v):
    b, s, _ = q.shape
    tq = min(s, ATTN_TILE)
    hp = MLA_HEADS_PER_STEP
    stat = pltpu.VMEM((hp, tq, LANES), jnp.float32)
    return pl.pallas_call(
        functools.partial(_mla_attn_kernel, tq=tq),
        grid=(b, A_HEADS // hp, s // tq),
        in_specs=[pl.BlockSpec((None, tq, hp * A_QK_PAD), lambda bi, h, i: (bi, i, h)),
                  pl.BlockSpec((None, s, hp * A_QK_PAD), lambda bi, h, i: (bi, 0, h),
                               pipeline_mode=pl.Buffered(1)),
                  pl.BlockSpec((None, s, hp * A_V), lambda bi, h, i: (bi, 0, h),
                               pipeline_mode=pl.Buffered(1)),
                  pl.BlockSpec((tq, tq), lambda bi, h, i: (0, 0))],
        out_specs=pl.BlockSpec((None, tq, hp * A_V), lambda bi, h, i: (bi, i, h)),
        out_shape=jax.ShapeDtypeStruct((b, s, A_WIDTH), jnp.float32),
        scratch_shapes=[stat, pltpu.VMEM((hp, tq, 2 * LANES), jnp.float32)],
        compiler_params=_params(3),
        name="mla_attn",
    )(q, k, v, _diag_bias(tq))


def _diff_attn_kernel(q_ref, k_ref, v_ref, c_ref, bias_ref, o_ref, qs_sc, m_sc, acc_sc, *, tq):
    i = pl.program_id(1)
    lane = lax.broadcasted_iota(jnp.int32, (tq, 2 * C_QK), 1)
    for h in range(C_HEADS):
        q = q_ref[:, h * 2 * C_QK:(h + 1) * 2 * C_QK]
        qs_sc[2 * h] = jnp.where(lane < C_QK, q, jnp.zeros_like(q))
        qs_sc[2 * h + 1] = jnp.where(lane >= C_QK, q, jnp.zeros_like(q))

    def tile(j, diagonal):
        rows = pl.ds(pl.multiple_of(j * tq, tq), tq)
        for h in range(C_HEADS):
            k = k_ref[rows, h * 2 * C_QK:(h + 1) * 2 * C_QK]
            v = v_ref[rows, h * C_V:(h + 1) * C_V]
            for t in range(2):
                n = 2 * h + t
                s = _dot_nt(qs_sc[n], k)
                if diagonal:
                    s = s + bias_ref[...]
                _online_softmax_step(s, v, m_sc.at[n], acc_sc.at[n], first=diagonal)

    def body(j, carry):
        tile(j, False)
        return carry

    tile(i, True)
    lax.fori_loop(0, i, body, 0)

    c = c_ref[...]
    lam_init = c[4:5, 0:1]
    lam = (jnp.exp(jnp.sum(c[0:1] * c[1:2], axis=-1, keepdims=True))
           - jnp.exp(jnp.sum(c[2:3] * c[3:4], axis=-1, keepdims=True)) + lam_init)
    for h in range(C_HEADS):
        o = _softmax_finish(acc_sc.at[2 * h]) - lam * _softmax_finish(acc_sc.at[2 * h + 1])
        o_ref[:, h * C_V:(h + 1) * C_V] = _rms(o, c[5:6]) * (1.0 - lam_init)


def _diff_attn(q, k, v, consts):
    b, s, _ = q.shape
    tq = min(s, ATTN_TILE)
    n_chain = 2 * C_HEADS
    stat = pltpu.VMEM((n_chain, tq, LANES), jnp.float32)
    resident = lambda width: pl.BlockSpec((None, s, width), lambda bi, i: (bi, 0, 0))
    return pl.pallas_call(
        functools.partial(_diff_attn_kernel, tq=tq),
        grid=(b, s // tq),
        in_specs=[pl.BlockSpec((None, tq, C_WIDTH), lambda bi, i: (bi, i, 0)),
                  resident(C_WIDTH), resident(C_WIDTH),
                  pl.BlockSpec((8, LANES), lambda bi, i: (0, 0)),
                  pl.BlockSpec((tq, tq), lambda bi, i: (0, 0))],
        out_specs=pl.BlockSpec((None, tq, C_WIDTH), lambda bi, i: (bi, i, 0)),
        out_shape=jax.ShapeDtypeStruct((b, s, C_WIDTH), jnp.float32),
        scratch_shapes=[pltpu.VMEM((n_chain, tq, 2 * C_QK), MXU_DTYPE), stat,
                        pltpu.VMEM((n_chain, tq, 2 * LANES), jnp.float32)],
        compiler_params=_params(2),
        name="diff_attn",
    )(q, k, v, consts, _diag_bias(tq))


SUBLANES = 8
COUNT_CHAINS = 4
PACKED_ROWS = 16
HALF_BITS = 16
HALF_MIN = -2 ** (HALF_BITS - 1)
IDX_GROUP = 4


def _tree_sum(parts):
    while len(parts) > 1:
        parts = [a + b for a, b in zip(parts[0::2], parts[1::2])] + parts[len(parts) & ~1:]
    return parts[0]


def _dsa_kernel(qb_ref, qi_ref, w_ref, ki_ref, kb_ref, vb_ref, o_ref,
                keys_sc, half_sc, qz_sc, m_sc, acc_sc, *, tq, n_sel):
    i = pl.program_id(1)
    tk = tq
    n_tiles = i + 1
    f32 = jnp.float32

    lane = lax.broadcasted_iota(jnp.int32, (tq, LANES), 1)
    for h in range(IDX_HEADS):
        pair = qi_ref[:, (h // 2) * LANES:(h // 2 + 1) * LANES]
        keep = (lane < IDX_DIM) if h % 2 == 0 else (lane >= IDX_DIM)
        qz_sc[h * tq:(h + 1) * tq, :] = jnp.where(keep, pair, jnp.zeros_like(pair))
    w_t = w_ref[...].T

    def score_tile(j, masked):
        rows = pl.ds(pl.multiple_of(j * tk, tk), tk)
        ki = ki_ref[rows, :]
        sc = jnp.zeros((tk, tq), f32)
        for h0 in range(0, IDX_HEADS, IDX_GROUP):
            rel = _dot_nt(ki, qz_sc[h0 * tq:(h0 + IDX_GROUP) * tq, :])
            for g in range(IDX_GROUP):
                sc = sc + jnp.maximum(rel[:, g * tq:(g + 1) * tq], 0.0) * w_t[h0 + g:h0 + g + 1, :]
        if masked:
            kc = (j * tk + lax.broadcasted_iota(jnp.int32, (tk, tq), 0)) // CHUNK
            qc = (i * tq + lax.broadcasted_iota(jnp.int32, (tk, tq), 1)) // CHUNK
            sc = jnp.where(kc <= qc, sc, NEG_INF)
        bits = lax.bitcast_convert_type(sc, jnp.int32)
        key = jnp.where(bits < 0, INT_MIN - bits, bits)
        key = jnp.where(sc > 0.5 * NEG_INF, key, INT_MIN)
        keys_sc[j] = key
        half_sc[j] = (key >> HALF_BITS).astype(jnp.int16)

    def score_body(j, carry):
        score_tile(j, False)
        return carry

    lax.fori_loop(0, n_tiles - 1, score_body, 0)
    score_tile(n_tiles - 1, True)

    def count_ge(t):
        def body(j, accs):
            accs = list(accs)
            for n, r in enumerate(range(0, tk, SUBLANES)):
                hit = jnp.where(keys_sc[j, r:r + SUBLANES, :] >= t, 1.0, 0.0)
                accs[n % COUNT_CHAINS] = accs[n % COUNT_CHAINS] + hit
            return tuple(accs)
        zero = jnp.zeros((SUBLANES, tq), f32)
        accs = lax.fori_loop(0, n_tiles, body, (zero,) * COUNT_CHAINS)
        return jnp.sum(_tree_sum(list(accs)), axis=0, keepdims=True)

    def count16(ref, t, strict):
        t16 = t.astype(jnp.int16)
        one, zero = jnp.ones((), jnp.int16), jnp.zeros((), jnp.int16)

        def body(j, accs):
            accs = list(accs)
            for n, r in enumerate(range(0, tk, PACKED_ROWS)):
                x = ref[j, r:r + PACKED_ROWS, :]
                hit = jnp.where((x > t16) if strict else (x >= t16), one, zero)
                accs[n % COUNT_CHAINS] = accs[n % COUNT_CHAINS] + hit
            return tuple(accs)

        accs = lax.fori_loop(0, n_tiles, body, (jnp.zeros((PACKED_ROWS, tq), jnp.int16),) * COUNT_CHAINS)
        return jnp.sum(_tree_sum(list(accs)).astype(jnp.int32), axis=0, keepdims=True)

    def kth_largest16(ref, k_need):
        def step(it, t):
            cand = t + lax.shift_left(jnp.int32(1), jnp.int32(HALF_BITS - 1) - it)
            return jnp.where(count16(ref, cand, False) >= k_need, cand, t)
        return lax.fori_loop(0, HALF_BITS, step, jnp.full((1, tq), HALF_MIN, jnp.int32))

    thr_hi = kth_largest16(half_sc, n_sel)
    k_lo = n_sel - count16(half_sc, thr_hi, True)
    thr_hi16 = thr_hi.astype(jnp.int16)

    def lo_body(j, carry):
        lo = ((keys_sc[j] & (2 ** HALF_BITS - 1)) + HALF_MIN).astype(jnp.int16)
        half_sc[j] = jnp.where(half_sc[j] == thr_hi16, lo, jnp.int16(HALF_MIN))
        return carry

    lax.fori_loop(0, n_tiles, lo_body, 0)
    thr_lo = kth_largest16(half_sc, k_lo)
    thr = thr_hi * 2 ** HALF_BITS + (thr_lo - HALF_MIN)
    thr = jnp.maximum(thr, INT_MIN + 1)
    k_f = float(n_sel)
    cnt_ge = count_ge(thr)

    @pl.when(jnp.max(cnt_ge) > k_f)
    def _():
        keep_n = k_f - count_ge(thr + 1)
        tri = (lax.broadcasted_iota(jnp.int32, (tk, tk), 0)
               >= lax.broadcasted_iota(jnp.int32, (tk, tk), 1)).astype(MXU_DTYPE)

        def body(j, carry):
            kt = keys_sc[j]
            eq = kt == thr
            rank = carry + _dot(tri, jnp.where(eq, 1.0, 0.0).astype(MXU_DTYPE))
            keys_sc[j] = jnp.where(eq & (rank > keep_n), INT_MIN, kt)
            return rank[tk - 1:tk, :]

        lax.fori_loop(0, n_tiles, body, jnp.zeros((1, tq), f32))

    def attn_tile(j, first):
        rows = pl.ds(j * tk if first else pl.multiple_of(j * tk, tk), tk)
        bias = jnp.where(keys_sc[j] >= thr, 0.0, NEG_INF).T
        for h in range(B_HEADS):
            sl = slice(h * HEAD_DIM, (h + 1) * HEAD_DIM)
            s = _dot_nt(qb_ref[:, sl], kb_ref[rows, sl]) + bias
            _online_softmax_step(s, vb_ref[rows, sl], m_sc.at[h], acc_sc.at[h], first=first)

    def attn_body(j, carry):
        attn_tile(j, False)
        return carry

    attn_tile(0, True)
    lax.fori_loop(1, n_tiles, attn_body, 0)
    for h in range(B_HEADS):
        o_ref[:, h * HEAD_DIM:(h + 1) * HEAD_DIM] = _softmax_finish(acc_sc.at[h])


def _dsa_attn(qb, qi, w, ki, kb, vb, n_sel):
    b, s, _ = qb.shape
    tq = min(s, ATTN_TILE)
    blk = lambda width: pl.BlockSpec((None, tq, width), lambda bi, i: (bi, i, 0))
    res = lambda width: pl.BlockSpec((None, s, width), lambda bi, i: (bi, 0, 0), pipeline_mode=pl.Buffered(1))
    return pl.pallas_call(
        functools.partial(_dsa_kernel, tq=tq, n_sel=n_sel),
        grid=(b, s // tq),
        in_specs=[blk(B_WIDTH), blk(IDX_HEADS * IDX_DIM), blk(LANES), res(LANES), res(B_WIDTH), res(B_WIDTH)],
        out_specs=blk(B_WIDTH),
        out_shape=jax.ShapeDtypeStruct((b, s, B_WIDTH), jnp.float32),
        scratch_shapes=[pltpu.VMEM((s // tq, tq, tq), jnp.int32),
                        pltpu.VMEM((s // tq, tq, tq), jnp.int16),
                        pltpu.VMEM((IDX_HEADS * tq, LANES), MXU_DTYPE),
                        pltpu.VMEM((B_HEADS, tq, LANES), jnp.float32),
                        pltpu.VMEM((B_HEADS, tq, 2 * LANES), jnp.float32)],
        compiler_params=_params(2),
        name="dsa_attn",
    )(qb, qi, w, ki, kb, vb)


def _out_kernel(oa_ref, ob_ref, oc_ref, g_ref, h_ref, p_ref, wo_ref, wpg_ref, wple_ref, gn_ref,
                *out_refs, emit_h):
    g = g_ref[...]
    sg = g * _sigmoid(g)
    h1 = h_ref[...]
    col = 0
    for o_ref in (oa_ref, ob_ref, oc_ref):
        width = o_ref.shape[1]
        mixed = (o_ref[...] * sg[:, col:col + width]).astype(MXU_DTYPE)
        h1 = h1 + _dot(mixed, wo_ref[col:col + width, :])
        col += width
    gate = _sigmoid(_dot(h1.astype(MXU_DTYPE), wpg_ref[...]))
    h2 = h1 + _dot(p_ref[...].astype(MXU_DTYPE), wple_ref[...]) * gate
    if emit_h:
        out_refs[0][...] = h2
    out_refs[-1][...] = _rms(h2, gn_ref[...]).astype(out_refs[-1].dtype)


def _out_proj(oa, ob, oc, g, h, p, wo, wpg, wple, layer, gn, emit_h):
    n, d = h.shape
    tm = min(n, OUT_ROWS)
    row = lambda a: pl.BlockSpec((tm, a.shape[1]), lambda i: (i, 0))
    const = lambda a: _layer_spec(a, layer, pipeline_mode=pl.Buffered(1))
    out_specs = [pl.BlockSpec((tm, d), lambda i: (i, 0))]
    out_shape = [jax.ShapeDtypeStruct((n, d), MXU_DTYPE if emit_h else jnp.float32)]
    if emit_h:
        out_specs = [pl.BlockSpec((tm, d), lambda i: (i, 0))] + out_specs
        out_shape = [jax.ShapeDtypeStruct((n, d), jnp.float32)] + out_shape
    return pl.pallas_call(
        functools.partial(_out_kernel, emit_h=emit_h),
        grid=(n // tm,),
        in_specs=[row(oa), row(ob), row(oc), row(g), row(h),
                  pl.BlockSpec((None, tm, p.shape[2]), lambda i: (layer, i, 0)),
                  const(wo), const(wpg), const(wple), pl.BlockSpec((1, d), lambda i: (0, 0))],
        out_specs=out_specs,
        out_shape=out_shape,
        compiler_params=_params(1),
        name="out_proj",
    )(oa, ob, oc, g, h, p, wo, wpg, wple, gn)


def _in_cols(name):
    k = IN_NAMES.index(name)
    start = sum(IN_SPLITS[:k])
    return start, start + IN_SPLITS[k]


_W_BLOCKS = (("c_q", "c_kv", "k_rope", LANES - A_ROPE),
             ("g_a", "g_b", "g_c"),
             ("q_b", "k_b", "v_b", "q_idx", "k_idx", "k_idx", "w_idx", LANES - IDX_HEADS),
             ("q_c", "k_c", "v_c"))


def _block_width(parts):
    return sum(p if isinstance(p, int) else IN_SPLITS[IN_NAMES.index(p)] for p in parts)


def _relayout_kernel(wt_ref, *out_refs):
    for parts, o_ref in zip(_W_BLOCKS, out_refs):
        row, pieces = 0, []
        for part in parts:
            if isinstance(part, int):
                pieces.append(jnp.zeros((part, wt_ref.shape[1]), jnp.float32))
            else:
                a, b = _in_cols(part)
                pieces.append(wt_ref[a:b, :])
            rows = sum(x.shape[0] for x in pieces)
            if rows % PACKED_ROWS == 0:
                o_ref[row:row + rows, :] = jnp.concatenate(pieces, axis=0).astype(o_ref.dtype)
                row, pieces = row + rows, []
        assert not pieces


def _relayout_w_in(w_in):
    depth, d, d_in = w_in.shape
    wt = jnp.transpose(w_in, (0, 2, 1))
    tc = RELAYOUT_COLS
    widths = [_block_width(parts) for parts in _W_BLOCKS]
    return pl.pallas_call(
        _relayout_kernel,
        grid=(depth, d // tc),
        in_specs=[pl.BlockSpec((None, d_in, tc), lambda l, c: (l, 0, c))],
        out_specs=[pl.BlockSpec((None, w, tc), lambda l, c: (l, 0, c)) for w in widths],
        out_shape=[jax.ShapeDtypeStruct((depth, w, d), MXU_DTYPE) for w in widths],
        compiler_params=_params(2),
        name="relayout_w_in",
    )(wt)


_SEG_G = ((0, D_MIX, None, None),)
_SEG_B = ((0, B_WIDTH, PAT_DSA, HEAD_DIM ** -0.5 * LOG2E), (B_WIDTH, B_WIDTH, PAT_DSA, None),
          (2 * B_WIDTH, B_WIDTH, None, None), (3 * B_WIDTH, IDX_HEADS * IDX_DIM, PAT_SMALL, None),
          (3 * B_WIDTH + IDX_HEADS * IDX_DIM, LANES, PAT_SMALL, None),
          (3 * B_WIDTH + IDX_HEADS * IDX_DIM + LANES, LANES, None, None))
_SEG_C = ((0, C_WIDTH, PAT_SMALL, C_QK ** -0.5 * LOG2E), (C_WIDTH, C_WIDTH, PAT_SMALL, None),
          (2 * C_WIDTH, C_WIDTH, None, None))


def kernel(x, p, positions, w_in, w_uq, w_ukv, w_o, norm_g, q_norm_g, kv_norm_g,
           lam_q1, lam_k1, lam_q2, lam_k2, subln_g, w_ple, w_pg, final_g):
    b, s, d = x.shape
    n = b * s
    f32 = jnp.float32
    n_sel = min(TOPK_MAX, s // 4)

    w_a, w_g, w_b, w_c = _relayout_w_in(w_in)
    uq = w_uq.reshape(DEPTH, A_Q_RANK, A_HEADS, A_NOPE + A_ROPE)
    uq = jnp.pad(uq, ((0, 0), (0, 0), (0, 0), (0, A_QK_PAD - A_NOPE - A_ROPE)))
    uq = uq.reshape(DEPTH, A_Q_RANK, A_HEADS * A_QK_PAD).astype(MXU_DTYPE)
    ukv = w_ukv.reshape(DEPTH, A_KV_RANK, A_HEADS, A_NOPE + A_V)
    ukv = jnp.concatenate([ukv[..., :A_NOPE].reshape(DEPTH, A_KV_RANK, A_WIDTH),
                           ukv[..., A_NOPE:].reshape(DEPTH, A_KV_RANK, A_WIDTH)], axis=-1).astype(MXU_DTYPE)
    w_o_c = w_o.astype(MXU_DTYPE)
    w_pg_c = w_pg.astype(MXU_DTYPE)
    w_ple_c = w_ple.astype(MXU_DTYPE)
    pad_lanes = lambda v: jnp.pad(v, (0, LANES - v.shape[0]))

    tabs = _rope_tables(positions.astype(f32).reshape(n, 1))
    h = x.reshape(n, d)
    u = _norm(h, norm_g[0], MXU_DTYPE)
    out = None
    for i in range(DEPTH):
        q_a, k_a, v_a = _mla_proj(u, w_a, i, q_norm_g[i].reshape(1, -1), kv_norm_g[i].reshape(1, -1),
                                  uq, ukv, tabs)
        (g,) = _proj(u, w_g, i, None, _SEG_G, (f32,), "proj_gates")
        q_b, k_b, v_b, q_i, k_i, w_i = _proj(u, w_b, i, tabs, _SEG_B,
                                             (MXU_DTYPE,) * 5 + (f32,), "proj_dsa")
        q_c, k_c, v_c = _proj(u, w_c, i, tabs, _SEG_C, (MXU_DTYPE,) * 3, "proj_diff")

        r3 = lambda a: a.reshape(b, s, a.shape[-1])
        o_a = _mla_attn(r3(q_a), r3(k_a), r3(v_a))
        o_b = _dsa_attn(r3(q_b), r3(q_i), r3(w_i), r3(k_i), r3(k_b), r3(v_b), n_sel)
        lam_init = 0.8 - 0.6 * math.exp(-0.3 * i)
        consts = jnp.stack([pad_lanes(lam_q1[i]), pad_lanes(lam_k1[i]), pad_lanes(lam_q2[i]),
                            pad_lanes(lam_k2[i]), jnp.full((LANES,), lam_init, f32), subln_g[i],
                            jnp.zeros((LANES,), f32), jnp.zeros((LANES,), f32)]).astype(f32)
        o_c = _diff_attn(r3(q_c), r3(k_c), r3(v_c), consts)

        last = i == DEPTH - 1
        gn = (final_g if last else norm_g[i + 1]).reshape(1, d)
        res = _out_proj(o_a.reshape(n, -1), o_b.reshape(n, -1), o_c.reshape(n, -1), g, h,
                        p.reshape(DEPTH, n, PLE_DIM), w_o_c, w_pg_c, w_ple_c, i, gn, emit_h=not last)
        if last:
            out = res[0]
        else:
            h, u = res
    return out.reshape(b, s, d)
```
